```python
import jax, jax.numpy as jnp
from jax import lax
import numpy as np

D_MODEL = 1024
BATCH = 16
SEQ = 2048
DEPTH = 4

GRID_W = 64
CTX_LEN = 256
EXPAND = 2
D_INNER = EXPAND * D_MODEL
M_HEADS = 4
M_HEAD_DIM = D_INNER // M_HEADS
CHUNK = 64
N_CONV_LAYERS = (DEPTH + 1) // 2
N_MLSTM_LAYERS = DEPTH // 2
EPS = 1e-6

kernel_name = "hybrid_shortconv_mlstm_dit"


def _rmsnorm(x, g):
    xf = x.astype(jnp.float32)
    y = xf * lax.rsqrt(jnp.mean(xf * xf, axis=-1, keepdims=True) + EPS)
    return (y * g.astype(jnp.float32)).astype(x.dtype)


def _modulate(x, g, shift, scale):
    return _rmsnorm(x, g) * (1 + scale) + shift


def _dwconv3(u, w):
    pad = [(0, 0)] * (u.ndim - 2) + [(1, 1), (0, 0)]
    p = jnp.pad(u, pad)
    return p[..., :-2, :] * w[0] + p[..., 1:-1, :] * w[1] + p[..., 2:, :] * w[2]


def _grid_conv3(u, w):
    b, s, ch = u.shape
    rows = s // GRID_W
    return _dwconv3(u.reshape(b, rows, GRID_W, ch), w).reshape(b, s, ch)


def _shortconv_mixer(h, w_in, w_conv, w_out, conv_fn):
    b_gate, c_gate, u, z = jnp.split(h @ w_in, 4, axis=-1)
    y = b_gate * conv_fn(c_gate * u, w_conv)
    return (y * jax.nn.silu(z)) @ w_out


def _zero_state(b):
    return (jnp.zeros((b, M_HEADS, M_HEAD_DIM, M_HEAD_DIM), jnp.float32),
            jnp.zeros((b, M_HEADS, M_HEAD_DIM), jnp.float32),
            jnp.zeros((b, M_HEADS), jnp.float32))


def _mlstm_scan(q, k, v, ig, lf, state):
    b, nh, t, dh = q.shape
    nc = t // CHUNK

    def to_chunks(a):
        return jnp.moveaxis(a.reshape(b, nh, nc, CHUNK, *a.shape[3:]), 2, 0)

    mask = jnp.tril(jnp.ones((CHUNK, CHUNK), bool))

    def step(carry, xs):
        c_prev, n_prev, m_prev = carry
        qc, kc, vc, ic, fc = xs
        bcum = jnp.cumsum(fc, axis=-1)
        dmat = bcum[..., :, None] - bcum[..., None, :] + ic[..., None, :]
        dmat = jnp.where(mask, dmat, -jnp.inf)
        m_inter = bcum + m_prev[..., None]
        m_j = jnp.maximum(m_inter, dmat.max(-1))
        s = jnp.einsum('bhjd,bhsd->bhjs', qc, kc) * jnp.exp(dmat - m_j[..., None])
        w_inter = jnp.exp(m_inter - m_j)
        num = w_inter[..., None] * jnp.einsum('bhjd,bhde->bhje', qc, c_prev) + jnp.einsum('bhjs,bhse->bhje', s, vc)
        den = w_inter * jnp.einsum('bhjd,bhd->bhj', qc, n_prev) + s.sum(-1)
        h = num / jnp.maximum(jnp.abs(den), jnp.exp(-m_j))[..., None]
        b_last = bcum[..., -1]
        g = b_last[..., None] - bcum + ic
        m_new = jnp.maximum(b_last + m_prev, g.max(-1))
        w_decay = jnp.exp(b_last + m_prev - m_new)
        w_s = jnp.exp(g - m_new[..., None])
        c_new = w_decay[..., None, None] * c_prev + jnp.einsum('bhs,bhsd,bhse->bhde', w_s, kc, vc)
        n_new = w_decay[..., None] * n_prev + jnp.einsum('bhs,bhsd->bhd', w_s, kc)
        return (c_new, n_new, m_new), h

    state, hs = lax.scan(step, state, (to_chunks(q), to_chunks(k), to_chunks(v), to_chunks(ig), to_chunks(lf)))
    h = jnp.moveaxis(hs, 0, 2).reshape(b, nh, t, dh)
    return h, state


def _mlstm_final_state(k, v, ig, lf):
    bcum = jnp.cumsum(lf, axis=-1)
    b_last = bcum[..., -1]
    g = b_last[..., None] - bcum + ig
    m = jnp.maximum(b_last, g.max(-1))
    w = jnp.exp(g - m[..., None])
    return (jnp.einsum('bht,bhtd,bhte->bhde', w, k, v), jnp.einsum('bht,bhtd->bhd', w, k), m)


def _flip_t(a):
    return jnp.flip(a, axis=2)


def _mlstm_inputs(u, w_conv, wq, wk, wv, w_gate, b_gate, conv_fn):
    b, t, _ = u.shape
    u_c = jax.nn.silu(conv_fn(u, w_conv))
    uc_h = u_c.reshape(b, t, M_HEADS, M_HEAD_DIM)
    u_h = u.reshape(b, t, M_HEADS, M_HEAD_DIM)
    q = jnp.einsum('bthd,hde->bthe', uc_h, wq)
    k = jnp.einsum('bthd,hde->bthe', uc_h, wk) * (M_HEAD_DIM ** -0.5)
    v = jnp.einsum('bthd,hde->bthe', u_h, wv)
    gin = jnp.concatenate([q.reshape(b, t, D_INNER), k.reshape(b, t, D_INNER), v.reshape(b, t, D_INNER)], axis=-1)
    gates = (gin @ w_gate + b_gate).astype(jnp.float32).reshape(b, t, 4, M_HEADS)
    gates = jnp.transpose(gates, (2, 0, 3, 1))
    to_bhtd = lambda a: jnp.transpose(a, (0, 2, 1, 3)).astype(jnp.float32)
    return to_bhtd(q), to_bhtd(k), to_bhtd(v), gates, u_c


def _mlstm_mixer(h, w_in, w_conv, wq, wk, wv, w_gate, b_gate, norm_g, skip, w_out, conv_fn, st_f, st_b):
    b, t, _ = h.shape
    u, z, o_pre = jnp.split(h @ w_in, 3, axis=-1)
    q, k, v, g, u_c = _mlstm_inputs(u, w_conv, wq, wk, wv, w_gate, b_gate, conv_fn)
    h_f, st_f = _mlstm_scan(q, k, v, g[0], jax.nn.log_sigmoid(g[1]), st_f)
    h_b, st_b = _mlstm_scan(_flip_t(q), _flip_t(k), _flip_t(v), _flip_t(g[2]),
                            jax.nn.log_sigmoid(_flip_t(g[3])), st_b)
    h_sum = h_f + _flip_t(h_b)
    mu = jnp.mean(h_sum, axis=-1, keepdims=True)
    var = jnp.mean(jnp.square(h_sum - mu), axis=-1, keepdims=True)
    hn = (h_sum - mu) * lax.rsqrt(var + EPS)
    hn = jnp.transpose(hn, (0, 2, 1, 3)).reshape(b, t, D_INNER).astype(h.dtype) * norm_g
    y = jax.nn.sigmoid(o_pre) * hn
    y = (y + skip * u_c) * jax.nn.silu(z)
    return y @ w_out, st_f, st_b


def _mlstm_context_states(h, w_in, w_conv, wq, wk, wv, w_gate, b_gate):
    u = h @ w_in[:, :D_INNER]
    q, k, v, g, _ = _mlstm_inputs(u, w_conv, wq, wk, wv, w_gate, b_gate, _dwconv3)
    st_f = _mlstm_final_state(k, v, g[0], jax.nn.log_sigmoid(g[1]))
    st_b = _mlstm_final_state(_flip_t(k), _flip_t(v), _flip_t(g[2]), jax.nn.log_sigmoid(_flip_t(g[3])))
    return st_f, st_b


def setup_inputs(seed: int = 0) -> dict:
    key = jax.random.key(seed)
    ks = jax.random.split(key, 32)
    f32 = jnp.float32
    nrm = lambda k, shape, scale: jax.random.normal(k, shape, f32) * scale
    nc, nm, h = N_CONV_LAYERS, N_MLSTM_LAYERS, M_HEADS
    f_bias = jnp.linspace(3.0, 6.0, h, dtype=f32)
    b_gate = jnp.concatenate([nrm(ks[20], (nm, h), 0.1), f_bias + nrm(ks[21], (nm, h), 0.1),
                              nrm(ks[22], (nm, h), 0.1), f_bias + nrm(ks[23], (nm, h), 0.1)], axis=-1)
    return {
        "x": nrm(ks[0], (BATCH, SEQ, D_MODEL), 1.0),
        "c": nrm(ks[1], (BATCH, D_MODEL), 1.0),
        "ctx": nrm(ks[2], (BATCH, CTX_LEN, D_MODEL), 1.0),
        "c_ctx": nrm(ks[3], (D_MODEL,), 1.0),
        "norm_g": 1.0 + nrm(ks[4], (DEPTH, D_MODEL), 0.05),
        "mod_w": nrm(ks[5], (DEPTH, D_MODEL, 3 * D_MODEL), 0.5 * D_MODEL ** -0.5),
        "mod_b": nrm(ks[6], (DEPTH, 3 * D_MODEL), 0.02),
        "conv_w_in": nrm(ks[7], (nc, D_MODEL, 4 * D_INNER), D_MODEL ** -0.5),
        "conv_w": nrm(ks[8], (nc, 3, D_INNER), 3 ** -0.5),
        "conv_w_out": nrm(ks[9], (nc, D_INNER, D_MODEL), D_INNER ** -0.5),
        "m_w_in": nrm(ks[10], (nm, D_MODEL, 3 * D_INNER), D_MODEL ** -0.5),
        "m_conv_w": nrm(ks[11], (nm, 3, D_INNER), 3 ** -0.5),
        "m_wq": nrm(ks[12], (nm, h, M_HEAD_DIM, M_HEAD_DIM), M_HEAD_DIM ** -0.5),
        "m_wk": nrm(ks[13], (nm, h, M_HEAD_DIM, M_HEAD_DIM), M_HEAD_DIM ** -0.5),
        "m_wv": nrm(ks[14], (nm, h, M_HEAD_DIM, M_HEAD_DIM), M_HEAD_DIM ** -0.5),
        "m_w_gate": nrm(ks[15], (nm, 3 * D_INNER, 4 * h), 0.3 * (3 * D_INNER) ** -0.5),
        "m_b_gate": b_gate,
        "m_norm_g": 1.0 + nrm(ks[16], (nm, D_INNER), 0.05),
        "m_skip": 1.0 + nrm(ks[17], (nm, D_INNER), 0.1),
        "m_w_out": nrm(ks[18], (nm, D_INNER, D_MODEL), D_INNER ** -0.5),
        "final_g": 1.0 + nrm(ks[19], (D_MODEL,), 0.05),
    }


def reference(x, c, ctx, c_ctx, norm_g, mod_w, mod_b, conv_w_in, conv_w, conv_w_out,
              m_w_in, m_conv_w, m_wq, m_wk, m_wv, m_w_gate, m_b_gate, m_norm_g, m_skip, m_w_out, final_g):
    b = x.shape[0]
    sc = jax.nn.silu(c)
    scc = jax.nn.silu(c_ctx)
    for i in range(DEPTH):
        last = i == DEPTH - 1
        j = i // 2
        shift, scale, gate = jnp.split(sc @ mod_w[i] + mod_b[i], 3, axis=-1)
        hx = _modulate(x, norm_g[i], shift[:, None], scale[:, None])
        if i % 2 == 0:
            x = x + gate[:, None] * _shortconv_mixer(hx, conv_w_in[j], conv_w[j], conv_w_out[j], _grid_conv3)
            if not last:
                shift_c, scale_c, gate_c = jnp.split(scc @ mod_w[i] + mod_b[i], 3, axis=-1)
                hc = _modulate(ctx, norm_g[i], shift_c, scale_c)
                ctx = ctx + gate_c * _shortconv_mixer(hc, conv_w_in[j], conv_w[j], conv_w_out[j], _dwconv3)
        else:
            shift_c, scale_c, gate_c = jnp.split(scc @ mod_w[i] + mod_b[i], 3, axis=-1)
            hc = _modulate(ctx, norm_g[i], shift_c, scale_c)
            if last:
                st_f, st_b = _mlstm_context_states(hc, m_w_in[j], m_conv_w[j], m_wq[j], m_wk[j], m_wv[j],
                                                   m_w_gate[j], m_b_gate[j])
            else:
                yc, st_f, st_b = _mlstm_mixer(hc, m_w_in[j], m_conv_w[j], m_wq[j], m_wk[j], m_wv[j],
                                              m_w_gate[j], m_b_gate[j], m_norm_g[j], m_skip[j], m_w_out[j],
                                              _dwconv3, _zero_state(b), _zero_state(b))
                ctx = ctx + gate_c * yc
            yx, _, _ = _mlstm_mixer(hx, m_w_in[j], m_conv_w[j], m_wq[j], m_wk[j], m_wv[j],
                                    m_w_gate[j], m_b_gate[j], m_norm_g[j], m_skip[j], m_w_out[j],
                                    _grid_conv3, st_f, st_b)
            x = x + gate[:, None] * yx
    return _rmsnorm(x, final_g)
```

```python
import functools

import jax
import jax.numpy as jnp
from jax import lax
from jax.experimental import pallas as pl
from jax.experimental.pallas import tpu as pltpu

D_MODEL = 1024
BATCH = 16
SEQ = 2048
DEPTH = 4
GRID_W = 64
CTX_LEN = 256
D_INNER = 2 * D_MODEL
HEADS = 4
HEAD_DIM = D_INNER // HEADS
EPS = 1e-6

SCAN_CHUNK = 256
COL_BLOCK = 512
ROW_TILE = 512
MOD_ROWS = 24
CTX_MOD_ROW = BATCH
GATE_LANES = 128
VMEM_LIMIT_BYTES = 56 * 1024 * 1024

F32 = jnp.float32
BF16 = jnp.bfloat16


def _dot(a, b):
    return jnp.dot(a, b, preferred_element_type=F32)


def _silu(x):
    return x * jax.nn.sigmoid(x)


def _params(*sem):
    return pltpu.CompilerParams(dimension_semantics=sem, vmem_limit_bytes=VMEM_LIMIT_BYTES)


def _mod_kernel(c_ref, w_ref, b_ref, o_ref):
    sc = _silu(c_ref[...])
    o_ref[...] = jnp.dot(sc, w_ref[...], preferred_element_type=F32,
                         precision=lax.Precision.HIGHEST) + b_ref[...]


def _modulation(c_all, mod_w, mod_b):
    nblk = 3 * D_MODEL // D_MODEL
    return pl.pallas_call(
        _mod_kernel,
        grid=(DEPTH, nblk),
        in_specs=[
            pl.BlockSpec((MOD_ROWS, D_MODEL), lambda l, j: (0, 0)),
            pl.BlockSpec((None, D_MODEL, D_MODEL), lambda l, j: (l, 0, j)),
            pl.BlockSpec((None, 1, D_MODEL), lambda l, j: (l, 0, j)),
        ],
        out_specs=pl.BlockSpec((None, MOD_ROWS, D_MODEL), lambda l, j: (l, 0, j)),
        out_shape=jax.ShapeDtypeStruct((DEPTH, MOD_ROWS, 3 * D_MODEL), F32),
        compiler_params=_params("arbitrary", "arbitrary"),
        name="modulation",
    )(c_all, mod_w, mod_b.reshape(DEPTH, 1, 3 * D_MODEL))


def _modulated_norm(x, g, mod):
    y = x * lax.rsqrt(jnp.mean(x * x, axis=-1, keepdims=True) + EPS) * g
    return y * (1.0 + mod[:, D_MODEL:2 * D_MODEL]) + mod[:, :D_MODEL]


def _conv3(u, w, period):
    rows = u.shape[0]
    t = lax.broadcasted_iota(jnp.int32, (rows, 1), 0) & (period - 1)
    prev = jnp.where(t != 0, pltpu.roll(u, 1, axis=0), 0.0)
    nxt = jnp.where(t != period - 1, pltpu.roll(u, rows - 1, axis=0), 0.0)
    return prev * w[0:1, :] + u * w[1:2, :] + nxt * w[2:3, :]


def _mod_index(tiles_per_batch):
    if tiles_per_batch is None:
        return lambda i, k: (CTX_MOD_ROW, 0, 0)
    return lambda i, k: (i // tiles_per_batch, 0, 0)


def _conv_layer_kernel(x_ref, mod_ref, g_ref, wb_ref, wc_ref, wu_ref, wz_ref, cw_ref, wo_ref,
                       o_ref, hx_ref, acc_ref, *, period):
    k = pl.program_id(1)

    @pl.when(k == 0)
    def _():
        hx_ref[...] = _modulated_norm(x_ref[...], g_ref[...], mod_ref[...]).astype(BF16)
        acc_ref[...] = jnp.zeros_like(acc_ref)

    hx = hx_ref[...]
    cu = _dot(hx, wc_ref[...]) * _dot(hx, wu_ref[...])
    y = _dot(hx, wb_ref[...]) * _conv3(cu, cw_ref[...], period)
    y = y * _silu(_dot(hx, wz_ref[...]))
    acc_ref[...] += _dot(y.astype(BF16), wo_ref[...])

    @pl.when(k == pl.num_programs(1) - 1)
    def _():
        o_ref[...] = x_ref[...] + mod_ref[:, 2 * D_MODEL:] * acc_ref[...]


def _conv_layer(xs, mod, norm_g, w_in, conv_w, w_out, *, period, tiles_per_batch):
    rows = xs.shape[0]
    nk = D_INNER // COL_BLOCK
    row_spec = pl.BlockSpec((ROW_TILE, D_MODEL), lambda i, k: (i, 0))
    w_in_spec = lambda g: pl.BlockSpec((D_MODEL, COL_BLOCK), lambda i, k: (0, g * nk + k))
    return pl.pallas_call(
        functools.partial(_conv_layer_kernel, period=period),
        grid=(rows // ROW_TILE, nk),
        in_specs=[
            row_spec,
            pl.BlockSpec((None, 1, 3 * D_MODEL), _mod_index(tiles_per_batch)),
            pl.BlockSpec((1, D_MODEL), lambda i, k: (0, 0)),
            w_in_spec(0), w_in_spec(1), w_in_spec(2), w_in_spec(3),
            pl.BlockSpec((3, COL_BLOCK), lambda i, k: (0, k)),
            pl.BlockSpec((COL_BLOCK, D_MODEL), lambda i, k: (k, 0)),
        ],
        out_specs=row_spec,
        out_shape=jax.ShapeDtypeStruct((rows, D_MODEL), F32),
        scratch_shapes=[pltpu.VMEM((ROW_TILE, D_MODEL), BF16), pltpu.VMEM((ROW_TILE, D_MODEL), F32)],
        compiler_params=_params("parallel", "arbitrary"),
        name="conv_layer",
    )(xs, mod, norm_g, w_in, w_in, w_in, w_in, conv_w, w_out)


def _mlstm_proj_kernel(x_ref, mod_ref, g_ref, wu_ref, wz_ref, wo_ref, cw_ref, wq_ref, wk_ref, wv_ref,
                       wg_ref, bg_ref, q_ref, k_ref, v_ref, gates_ref, *rest, period, gated):
    hx_ref = rest[-1]
    h = pl.program_id(1)

    @pl.when(h == 0)
    def _():
        hx_ref[...] = _modulated_norm(x_ref[...], g_ref[...], mod_ref[...]).astype(BF16)
        gates_ref[...] = jnp.broadcast_to(bg_ref[...], gates_ref.shape)

    hx = hx_ref[...]
    u = _dot(hx, wu_ref[...])
    uc = _silu(_conv3(u, cw_ref[...], period))
    ucb = uc.astype(BF16)
    qb = _dot(ucb, wq_ref[...]).astype(BF16)
    kb = (_dot(ucb, wk_ref[...]) * (HEAD_DIM ** -0.5)).astype(BF16)
    vb = _dot(u.astype(BF16), wv_ref[...]).astype(BF16)
    q_ref[...] = qb
    k_ref[...] = kb
    v_ref[...] = vb
    gates_ref[...] += _dot(qb, wg_ref[0]) + _dot(kb, wg_ref[1]) + _dot(vb, wg_ref[2])
    if gated:
        uc_ref, z_ref, o_ref = rest[:3]
        uc_ref[...] = ucb
        z_ref[...] = _dot(hx, wz_ref[...]).astype(BF16)
        o_ref[...] = _dot(hx, wo_ref[...]).astype(BF16)


def _mlstm_proj(xs, mod, norm_g, w_in, conv_w, wq, wk, wv, wg, bg, *, period, tiles_per_batch, gated):
    rows = xs.shape[0]
    w_in_spec = lambda g: pl.BlockSpec((D_MODEL, HEAD_DIM), lambda i, h: (0, g * HEADS + h))
    head_w = pl.BlockSpec((None, HEAD_DIM, HEAD_DIM), lambda i, h: (h, 0, 0))
    head_out = pl.BlockSpec((ROW_TILE, HEAD_DIM), lambda i, h: (i, h))
    inner = jax.ShapeDtypeStruct((rows, D_INNER), BF16)
    n_inner = 6 if gated else 3
    outs = pl.pallas_call(
        functools.partial(_mlstm_proj_kernel, period=period, gated=gated),
        grid=(rows // ROW_TILE, HEADS),
        in_specs=[
            pl.BlockSpec((ROW_TILE, D_MODEL), lambda i, h: (i, 0)),
            pl.BlockSpec((None, 1, 3 * D_MODEL), _mod_index(tiles_per_batch)),
            pl.BlockSpec((1, D_MODEL), lambda i, h: (0, 0)),
            w_in_spec(0), w_in_spec(1), w_in_spec(2),
            pl.BlockSpec((3, HEAD_DIM), lambda i, h: (0, h)),
            head_w, head_w, head_w,
            pl.BlockSpec((3, None, HEAD_DIM, 2 * GATE_LANES), lambda i, h: (0, h, 0, 0)),
            pl.BlockSpec((1, 2 * GATE_LANES), lambda i, h: (0, 0)),
        ],
        out_specs=[head_out] * 3 + [pl.BlockSpec((ROW_TILE, 2 * GATE_LANES), lambda i, h: (i, 0))]
        + [head_out] * (n_inner - 3),
        out_shape=[inner] * 3 + [jax.ShapeDtypeStruct((rows, 2 * GATE_LANES), F32)] + [inner] * (n_inner - 3),
        scratch_shapes=[pltpu.VMEM((ROW_TILE, D_MODEL), BF16)],
        compiler_params=_params("parallel", "arbitrary"),
        name="mlstm_proj",
    )(xs, mod, norm_g, w_in, w_in, w_in, conv_w, wq, wk, wv, wg, bg)
    return outs


def _log_sigmoid(x):
    return jnp.minimum(x, 0.0) - jnp.log1p(jnp.exp(-jnp.abs(x)))


def _split3(x):
    hi = x.astype(BF16)
    r = x - hi.astype(F32)
    mid = r.astype(BF16)
    lo = (r - mid.astype(F32)).astype(BF16)
    return hi, mid, lo


def _gate_prep_kernel(gcol_ref, grow_ref, bc_ref, gc_ref, cr_ref, blr_ref, *, nchunks):
    L = SCAN_CHUNK
    r = lax.broadcasted_iota(jnp.int32, (L, L), 0)
    c = lax.broadcasted_iota(jnp.int32, (L, L), 1)
    lower = (c <= r).astype(BF16)
    upper = (c >= r).astype(BF16)
    fwd_lane = lax.broadcasted_iota(jnp.int32, (1, GATE_LANES), 1) < HEADS
    fwd_row = lax.broadcasted_iota(jnp.int32, (2 * HEADS, 1), 0) < HEADS
    for ci in range(nchunks):
        rows = pl.ds(ci * L, L)
        ig = gcol_ref[rows, :GATE_LANES]
        parts = _split3(_log_sigmoid(gcol_ref[rows, GATE_LANES:]))
        pre = sum(_dot(lower, p) for p in parts)
        suf = sum(_dot(upper, p) for p in parts)
        bcum = jnp.where(fwd_lane, pre, suf)
        b_last = jnp.where(fwd_lane, pre[L - 1:L, :], suf[0:1, :])
        bc_ref[rows, :] = bcum
        gc_ref[rows, :] = b_last - bcum + ig
        ig_r = grow_ref[0:2 * HEADS, rows]
        parts_r = _split3(_log_sigmoid(grow_ref[:, rows]))
        pre_r = sum(_dot(p, upper) for p in parts_r)[2 * HEADS:, :]
        suf_r = sum(_dot(p, lower) for p in parts_r)[2 * HEADS:, :]
        cr_ref[ci] = ig_r - jnp.where(fwd_row, pre_r, suf_r)
        b_last_r = jnp.where(fwd_row, pre_r[:, L - 1:L], suf_r[:, 0:1])
        blr_ref[ci] = jnp.broadcast_to(b_last_r, (2 * HEADS, GATE_LANES))


def _gate_prep(gates_col, gates_row, tokens):
    nchunks = tokens // SCAN_CHUNK
    col = pl.BlockSpec((None, tokens, GATE_LANES), lambda b: (b, 0, 0))
    return pl.pallas_call(
        functools.partial(_gate_prep_kernel, nchunks=nchunks),
        grid=(BATCH,),
        in_specs=[
            pl.BlockSpec((None, tokens, 2 * GATE_LANES), lambda b: (b, 0, 0)),
            pl.BlockSpec((None, 4 * HEADS, tokens), lambda b: (b, 0, 0)),
        ],
        out_specs=[
            col, col,
            pl.BlockSpec((None, nchunks, 2 * HEADS, SCAN_CHUNK), lambda b: (b, 0, 0, 0)),
            pl.BlockSpec((None, nchunks, 2 * HEADS, GATE_LANES), lambda b: (b, 0, 0, 0)),
        ],
        out_shape=[
            jax.ShapeDtypeStruct((BATCH, tokens, GATE_LANES), F32),
            jax.ShapeDtypeStruct((BATCH, tokens, GATE_LANES), F32),
            jax.ShapeDtypeStruct((BATCH, nchunks, 2 * HEADS, SCAN_CHUNK), F32),
            jax.ShapeDtypeStruct((BATCH, nchunks, 2 * HEADS, GATE_LANES), F32),
        ],
        compiler_params=_params("parallel"),
        name="gate_prep",
    )(gates_col, gates_row)


class _Seq:
    def __init__(self, q, k, v, bc, gc, cr, blr, hs, hn, nchunks):
        self.q, self.k, self.v, self.bc, self.gc, self.cr, self.blr = q, k, v, bc, gc, cr, blr
        self.hs, self.hn, self.nchunks = hs, hn, nchunks


def _scan_chunk(seq, ci, direction, head, c_ref, n, m):
    L = SCAN_CHUNK
    rows = pl.ds(pl.multiple_of(ci * L, L), L)
    gate = direction * HEADS + head
    lane_sel = lax.broadcasted_iota(jnp.int32, (1, GATE_LANES), 1) == gate
    pick = lambda ref: jnp.sum(jnp.where(lane_sel, ref[rows, :], 0.0), axis=-1, keepdims=True)
    q = seq.q[rows, :]
    k = seq.k[rows, :]
    v = seq.v[rows, :]
    bcum = pick(seq.bc)
    g = pick(seq.gc)
    b_last = seq.blr[ci, pl.ds(gate, 1), :][:, 0:1]

    if seq.hn is not None:
        j = lax.broadcasted_iota(jnp.int32, (L, L), 0)
        s = lax.broadcasted_iota(jnp.int32, (L, L), 1)
        seen = (s <= j) if direction == 0 else (s >= j)
        dmat = jnp.where(seen, bcum + seq.cr[ci, pl.ds(gate, 1), :], -jnp.inf)
        m_inter = bcum + m
        m_j = jnp.maximum(m_inter, jnp.max(dmat, axis=-1, keepdims=True))
        sc = lax.dot_general(q, k, (((1,), (1,)), ((), ())), preferred_element_type=F32)
        sc = sc * jnp.exp(dmat - m_j)
        w_inter = jnp.exp(m_inter - m_j)
        num = w_inter * _dot(q, c_ref[...].astype(BF16)) + _dot(sc.astype(BF16), v)
        den = (w_inter * jnp.sum(q.astype(F32) * n, axis=-1, keepdims=True)
               + jnp.sum(sc, axis=-1, keepdims=True))
        h = num / jnp.maximum(jnp.abs(den), jnp.exp(-m_j))
        if direction == 0:
            seq.hs[rows, :] = h
        else:
            hsum = seq.hs[rows, :] + h
            hc = hsum - jnp.mean(hsum, axis=-1, keepdims=True)
            var = jnp.mean(hc * hc, axis=-1, keepdims=True)
            seq.hn[rows, :] = (hc * lax.rsqrt(var + EPS)).astype(BF16)

    m_new = jnp.maximum(b_last + m, jnp.max(g, axis=0, keepdims=True))
    w_decay = jnp.exp(b_last + m - m_new)
    kw = k.astype(F32) * jnp.exp(g - m_new)
    c_ref[...] = w_decay * c_ref[...] + lax.dot_general(
        kw.astype(BF16), v, (((0,), (0,)), ((), ())), preferred_element_type=F32)
    n_new = w_decay * n + jnp.sum(kw, axis=0, keepdims=True)
    return n_new, m_new


def _scan_kernel(*refs, ctx_out):
    (qx, kx, vx, bcx, gcx, crx, blrx, qc, kc, vc, bcc, gcc, crc, blrc) = refs[:14]
    if ctx_out:
        hnx, hnc, c_ref, hsx, hsc = refs[14:]
    else:
        hnx, c_ref, hsx = refs[14:]
        hnc = hsc = None
    head = pl.program_id(1)
    ctx = _Seq(qc, kc, vc, bcc, gcc, crc, blrc, hsc, hnc, CTX_LEN // SCAN_CHUNK)
    lat = _Seq(qx, kx, vx, bcx, gcx, crx, blrx, hsx, hnx, SEQ // SCAN_CHUNK)
    for direction in (0, 1):
        c_ref[...] = jnp.zeros_like(c_ref)
        state = (jnp.zeros((1, HEAD_DIM), F32), jnp.zeros((1, 1), F32))
        for seq in (ctx, lat):
            def body(i, st, seq=seq):
                ci = i if direction == 0 else seq.nchunks - 1 - i
                return _scan_chunk(seq, ci, direction, head, c_ref, *st)
            state = lax.fori_loop(0, seq.nchunks, body, state)


def _scan(lat, ctx, *, ctx_out):
    def specs(tokens):
        nch = tokens // SCAN_CHUNK
        qkv = pl.BlockSpec((None, tokens, HEAD_DIM), lambda b, h: (b, 0, h))
        col = pl.BlockSpec((None, tokens, GATE_LANES), lambda b, h: (b, 0, 0))
        return [qkv, qkv, qkv, col, col,
                pl.BlockSpec((None, nch, 2 * HEADS, SCAN_CHUNK), lambda b, h: (b, 0, 0, 0)),
                pl.BlockSpec((None, nch, 2 * HEADS, GATE_LANES), lambda b, h: (b, 0, 0, 0))]

    hn_spec = lambda tokens: pl.BlockSpec((None, tokens, HEAD_DIM), lambda b, h: (b, 0, h))
    hn_shape = lambda tokens: jax.ShapeDtypeStruct((BATCH, tokens, D_INNER), BF16)
    out_specs = [hn_spec(SEQ)] + ([hn_spec(CTX_LEN)] if ctx_out else [])
    out_shape = [hn_shape(SEQ)] + ([hn_shape(CTX_LEN)] if ctx_out else [])
    scratch = [pltpu.VMEM((HEAD_DIM, HEAD_DIM), F32), pltpu.VMEM((SEQ, HEAD_DIM), F32)]
    if ctx_out:
        scratch.append(pltpu.VMEM((CTX_LEN, HEAD_DIM), F32))
    outs = pl.pallas_call(
        functools.partial(_scan_kernel, ctx_out=ctx_out),
        grid=(BATCH, HEADS),
        in_specs=specs(SEQ) + specs(CTX_LEN),
        out_specs=out_specs,
        out_shape=out_shape,
        scratch_shapes=scratch,
        compiler_params=_params("parallel", "arbitrary"),
        name="mlstm_scan",
    )(*lat, *ctx)
    return outs if ctx_out else (outs[0], None)


def _mlstm_out_kernel(x_ref, mod_ref, hn_ref, uc_ref, z_ref, o_ref, ng_ref, sk_ref, wo_ref, fg_ref,
                      out_ref, acc_ref, *, final):
    k = pl.program_id(1)

    @pl.when(k == 0)
    def _():
        acc_ref[...] = jnp.zeros_like(acc_ref)

    y = jax.nn.sigmoid(o_ref[...].astype(F32)) * (hn_ref[...].astype(F32) * ng_ref[...])
    y = (y + sk_ref[...] * uc_ref[...].astype(F32)) * _silu(z_ref[...].astype(F32))
    acc_ref[...] += _dot(y.astype(BF16), wo_ref[...])

    @pl.when(k == pl.num_programs(1) - 1)
    def _():
        xn = x_ref[...] + mod_ref[:, 2 * D_MODEL:] * acc_ref[...]
        if final:
            xn = xn * lax.rsqrt(jnp.mean(xn * xn, axis=-1, keepdims=True) + EPS) * fg_ref[...]
        out_ref[...] = xn


def _mlstm_out(xs, mod, hn, uc, z, o, norm_g, skip, w_out, final_g, *, tiles_per_batch, final):
    rows = xs.shape[0]
    nk = D_INNER // COL_BLOCK
    row_spec = pl.BlockSpec((ROW_TILE, D_MODEL), lambda i, k: (i, 0))
    inner = pl.BlockSpec((ROW_TILE, COL_BLOCK), lambda i, k: (i, k))
    vec = pl.BlockSpec((1, COL_BLOCK), lambda i, k: (0, k))
    return pl.pallas_call(
        functools.partial(_mlstm_out_kernel, final=final),
        grid=(rows // ROW_TILE, nk),
        in_specs=[
            row_spec,
            pl.BlockSpec((None, 1, 3 * D_MODEL), _mod_index(tiles_per_batch)),
            inner, inner, inner, inner, vec, vec,
            pl.BlockSpec((COL_BLOCK, D_MODEL), lambda i, k: (k, 0)),
            pl.BlockSpec((1, D_MODEL), lambda i, k: (0, 0)),
        ],
        out_specs=row_spec,
        out_shape=jax.ShapeDtypeStruct((rows, D_MODEL), F32),
        scratch_shapes=[pltpu.VMEM((ROW_TILE, D_MODEL), F32)],
        compiler_params=_params("parallel", "arbitrary"),
        name="mlstm_out",
    )(xs, mod, hn, uc, z, o, norm_g, skip, w_out, final_g)


def _gate_weights(w_gate, b_gate):
    def place(a):
        a = a.reshape(a.shape[:-1] + (4, HEADS))
        ig = a[..., 0::2, :].reshape(a.shape[:-2] + (2 * HEADS,))
        fg = a[..., 1::2, :].reshape(a.shape[:-2] + (2 * HEADS,))
        pad = [(0, 0)] * (ig.ndim - 1) + [(0, GATE_LANES - 2 * HEADS)]
        return jnp.concatenate([jnp.pad(ig, pad), jnp.pad(fg, pad)], axis=-1)

    wg = place(w_gate).reshape(3, HEADS, HEAD_DIM, 2 * GATE_LANES).astype(BF16)
    bg = place(b_gate).reshape(1, 2 * GATE_LANES)
    return wg, bg


def _gate_layouts(gates, tokens):
    col = gates.reshape(BATCH, tokens, 2 * GATE_LANES)
    used = jnp.concatenate([col[..., :2 * HEADS], col[..., GATE_LANES:GATE_LANES + 2 * HEADS]], axis=-1)
    return col, jnp.swapaxes(used, 1, 2)


def kernel(x, c, ctx, c_ctx, norm_g, mod_w, mod_b, conv_w_in, conv_w, conv_w_out, m_w_in, m_conv_w, m_wq,
           m_wk, m_wv, m_w_gate, m_b_gate, m_norm_g, m_skip, m_w_out, final_g):
    assert x.shape == (BATCH, SEQ, D_MODEL) and ctx.shape == (BATCH, CTX_LEN, D_MODEL)
    c_all = jnp.zeros((MOD_ROWS, D_MODEL), F32).at[:BATCH].set(c).at[CTX_MOD_ROW].set(c_ctx)
    mods = _modulation(c_all, mod_w, mod_b).reshape(DEPTH, MOD_ROWS, 1, 3 * D_MODEL)
    xs = x.reshape(BATCH * SEQ, D_MODEL)
    cs = ctx.reshape(BATCH * CTX_LEN, D_MODEL)
    lat_tiles = SEQ // ROW_TILE
    final_g2 = final_g.reshape(1, D_MODEL)
    for i in range(DEPTH):
        last = i == DEPTH - 1
        j = i // 2
        g = norm_g[i].reshape(1, D_MODEL)
        if i % 2 == 0:
            w_in, w_out = conv_w_in[j].astype(BF16), conv_w_out[j].astype(BF16)
            xs = _conv_layer(xs, mods[i], g, w_in, conv_w[j], w_out, period=GRID_W, tiles_per_batch=lat_tiles)
            if not last:
                cs = _conv_layer(cs, mods[i], g, w_in, conv_w[j], w_out, period=CTX_LEN, tiles_per_batch=None)
        else:
            w_in, w_out = m_w_in[j].astype(BF16), m_w_out[j].astype(BF16)
            wq, wk, wv = m_wq[j].astype(BF16), m_wk[j].astype(BF16), m_wv[j].astype(BF16)
            wg, bg = _gate_weights(m_w_gate[j], m_b_gate[j])
            proj = functools.partial(_mlstm_proj, norm_g=g, w_in=w_in, conv_w=m_conv_w[j], wq=wq, wk=wk, wv=wv,
                                     wg=wg, bg=bg)
            px = proj(xs, mods[i], period=GRID_W, tiles_per_batch=lat_tiles, gated=True)
            pc = proj(cs, mods[i], period=CTX_LEN, tiles_per_batch=None, gated=not last)

            def stream(p, tokens):
                qkv = [a.reshape(BATCH, tokens, D_INNER) for a in p[:3]]
                return qkv + list(_gate_prep(*_gate_layouts(p[3], tokens), tokens))

            hnx, hnc = _scan(stream(px, SEQ), stream(pc, CTX_LEN), ctx_out=not last)
            out = functools.partial(_mlstm_out, norm_g=m_norm_g[j].reshape(1, D_INNER),
                                    skip=m_skip[j].reshape(1, D_INNER), w_out=w_out, final_g=final_g2)
            if not last:
                cs = out(cs, mods[i], hnc.reshape(-1, D_INNER), *pc[4:], tiles_per_batch=None, final=False)
            xs = out(xs, mods[i], hnx.reshape(-1, D_INNER), *px[4:], tiles_per_batch=lat_tiles, final=last)
    return xs.reshape(BATCH, SEQ, D_MODEL)
```

```python
import functools

import jax
import jax.numpy as jnp
from jax import lax
from jax.experimental import pallas as pl
from jax.experimental.pallas import tpu as pltpu

D_MODEL = 1024
BATCH = 16
SEQ = 2048
DEPTH = 4
GRID_W = 64
CTX_LEN = 256
D_INNER = 2 * D_MODEL
HEADS = 4
HEAD_DIM = D_INNER // HEADS
EPS = 1e-6

SCAN_CHUNK = 256
COL_BLOCK = 512
ROW_TILE = 1024
SUB_ROWS = 512
MOD_ROWS = 24
CTX_MOD_ROW = BATCH
GATE_LANES = 128
VMEM_LIMIT_BYTES = 56 * 1024 * 1024

F32 = jnp.float32
BF16 = jnp.bfloat16


def _dot(a, b):
    return jnp.dot(a, b, preferred_element_type=F32)


def _sigmoid(x):
    return 0.5 * jnp.tanh(0.5 * x) + 0.5


def _silu(x):
    return x * _sigmoid(x)


def _params(*sem):
    return pltpu.CompilerParams(dimension_semantics=sem, vmem_limit_bytes=VMEM_LIMIT_BYTES)


def _mod_kernel(c_ref, w_ref, b_ref, o_ref):
    sc = _silu(c_ref[...])
    o_ref[...] = jnp.dot(sc, w_ref[...], preferred_element_type=F32,
                         precision=lax.Precision.HIGHEST) + b_ref[...]


def _modulation(c_all, mod_w, mod_b):
    nblk = 3 * D_MODEL // D_MODEL
    return pl.pallas_call(
        _mod_kernel,
        grid=(DEPTH, nblk),
        in_specs=[
            pl.BlockSpec((MOD_ROWS, D_MODEL), lambda l, j: (0, 0)),
            pl.BlockSpec((None, D_MODEL, D_MODEL), lambda l, j: (l, 0, j)),
            pl.BlockSpec((None, 1, D_MODEL), lambda l, j: (l, 0, j)),
        ],
        out_specs=pl.BlockSpec((None, MOD_ROWS, D_MODEL), lambda l, j: (l, 0, j)),
        out_shape=jax.ShapeDtypeStruct((DEPTH, MOD_ROWS, 3 * D_MODEL), F32),
        compiler_params=_params("arbitrary", "arbitrary"),
        name="modulation",
    )(c_all, mod_w, mod_b.reshape(DEPTH, 1, 3 * D_MODEL))


def _modulated_norm(x, g, mod):
    y = x * lax.rsqrt(jnp.mean(x * x, axis=-1, keepdims=True) + EPS) * g
    return y * (1.0 + mod[:, D_MODEL:2 * D_MODEL]) + mod[:, :D_MODEL]


def _conv3(u, w, period):
    rows = u.shape[0]
    t = lax.broadcasted_iota(jnp.int32, (rows, 1), 0) & (period - 1)
    prev = jnp.where(t != 0, pltpu.roll(u, 1, axis=0), 0.0)
    nxt = jnp.where(t != period - 1, pltpu.roll(u, rows - 1, axis=0), 0.0)
    return prev * w[0:1, :] + u * w[1:2, :] + nxt * w[2:3, :]


def _sub_tiles():
    return [pl.ds(r * SUB_ROWS, SUB_ROWS) for r in range(ROW_TILE // SUB_ROWS)]


def _mod_index(tiles_per_batch):
    if tiles_per_batch is None:
        return lambda i, k: (CTX_MOD_ROW, 0, 0)
    return lambda i, k: (i // tiles_per_batch, 0, 0)


def _conv_layer_kernel(x_ref, mod_ref, g_ref, wb_ref, wc_ref, wu_ref, wz_ref, cw_ref, wo_ref,
                       o_ref, hx_ref, acc_ref, *, period):
    k = pl.program_id(1)

    @pl.when(k == 0)
    def _():
        for rows in _sub_tiles():
            hx_ref[rows, :] = _modulated_norm(x_ref[rows, :], g_ref[...], mod_ref[...]).astype(BF16)
        acc_ref[...] = jnp.zeros_like(acc_ref)

    for rows in _sub_tiles():
        hx = hx_ref[rows, :]
        cu = _dot(hx, wc_ref[...]) * _dot(hx, wu_ref[...])
        y = _dot(hx, wb_ref[...]) * _conv3(cu, cw_ref[...], period)
        y = y * _silu(_dot(hx, wz_ref[...]))
        acc_ref[rows, :] += _dot(y.astype(BF16), wo_ref[...])

    @pl.when(k == pl.num_programs(1) - 1)
    def _():
        for rows in _sub_tiles():
            o_ref[rows, :] = x_ref[rows, :] + mod_ref[:, 2 * D_MODEL:] * acc_ref[rows, :]


def _conv_layer(xs, mod, norm_g, w_in, conv_w, w_out, *, period, tiles_per_batch):
    rows = xs.shape[0]
    nk = D_INNER // COL_BLOCK
    row_spec = pl.BlockSpec((ROW_TILE, D_MODEL), lambda i, k: (i, 0))
    w_in_spec = lambda g: pl.BlockSpec((D_MODEL, COL_BLOCK), lambda i, k: (0, g * nk + k))
    return pl.pallas_call(
        functools.partial(_conv_layer_kernel, period=period),
        grid=(rows // ROW_TILE, nk),
        in_specs=[
            row_spec,
            pl.BlockSpec((None, 1, 3 * D_MODEL), _mod_index(tiles_per_batch)),
            pl.BlockSpec((1, D_MODEL), lambda i, k: (0, 0)),
            w_in_spec(0), w_in_spec(1), w_in_spec(2), w_in_spec(3),
            pl.BlockSpec((3, COL_BLOCK), lambda i, k: (0, k)),
            pl.BlockSpec((COL_BLOCK, D_MODEL), lambda i, k: (k, 0)),
        ],
        out_specs=row_spec,
        out_shape=jax.ShapeDtypeStruct((rows, D_MODEL), F32),
        scratch_shapes=[pltpu.VMEM((ROW_TILE, D_MODEL), BF16), pltpu.VMEM((ROW_TILE, D_MODEL), F32)],
        compiler_params=_params("parallel", "arbitrary"),
        name="conv_layer",
    )(xs, mod, norm_g, w_in, w_in, w_in, w_in, conv_w, w_out)


def _mlstm_proj_kernel(x_ref, mod_ref, g_ref, wu_ref, wz_ref, wo_ref, cw_ref, wq_ref, wk_ref, wv_ref,
                       wg_ref, bg_ref, ng_ref, sk_ref, q_ref, k_ref, v_ref, gates_ref, *rest, period, gated):
    hx_ref = rest[-1]
    h = pl.program_id(1)

    @pl.when(h == 0)
    def _():
        for rows in _sub_tiles():
            hx_ref[rows, :] = _modulated_norm(x_ref[rows, :], g_ref[...], mod_ref[...]).astype(BF16)
        gates_ref[...] = jnp.broadcast_to(bg_ref[...], gates_ref.shape)

    for rows in _sub_tiles():
        hx = hx_ref[rows, :]
        u = _dot(hx, wu_ref[...])
        uc = _silu(_conv3(u, cw_ref[...], period))
        ucb = uc.astype(BF16)
        qb = _dot(ucb, wq_ref[...]).astype(BF16)
        kb = (_dot(ucb, wk_ref[...]) * (HEAD_DIM ** -0.5)).astype(BF16)
        vb = _dot(u.astype(BF16), wv_ref[...]).astype(BF16)
        q_ref[rows, :] = qb
        k_ref[rows, :] = kb
        v_ref[rows, :] = vb
        gates_ref[rows, :] += _dot(qb, wg_ref[0]) + _dot(kb, wg_ref[1]) + _dot(vb, wg_ref[2])
        if gated:
            a_ref, b_ref = rest[:2]
            sz = _silu(_dot(hx, wz_ref[...]))
            a_ref[rows, :] = (_sigmoid(_dot(hx, wo_ref[...])) * ng_ref[...] * sz).astype(BF16)
            b_ref[rows, :] = (sk_ref[...] * uc * sz).astype(BF16)


def _mlstm_proj(xs, mod, norm_g, w_in, conv_w, wq, wk, wv, wg, bg, m_norm_g, m_skip, *, period,
                tiles_per_batch, gated):
    rows = xs.shape[0]
    w_in_spec = lambda g: pl.BlockSpec((D_MODEL, HEAD_DIM), lambda i, h: (0, g * HEADS + h))
    head_w = pl.BlockSpec((None, HEAD_DIM, HEAD_DIM), lambda i, h: (h, 0, 0))
    head_out = pl.BlockSpec((ROW_TILE, HEAD_DIM), lambda i, h: (i, h))
    head_vec = pl.BlockSpec((1, HEAD_DIM), lambda i, h: (0, h))
    inner = jax.ShapeDtypeStruct((rows, D_INNER), BF16)
    n_inner = 5 if gated else 3
    outs = pl.pallas_call(
        functools.partial(_mlstm_proj_kernel, period=period, gated=gated),
        grid=(rows // ROW_TILE, HEADS),
        in_specs=[
            pl.BlockSpec((ROW_TILE, D_MODEL), lambda i, h: (i, 0)),
            pl.BlockSpec((None, 1, 3 * D_MODEL), _mod_index(tiles_per_batch)),
            pl.BlockSpec((1, D_MODEL), lambda i, h: (0, 0)),
            w_in_spec(0), w_in_spec(1), w_in_spec(2),
            pl.BlockSpec((3, HEAD_DIM), lambda i, h: (0, h)),
            head_w, head_w, head_w,
            pl.BlockSpec((3, None, HEAD_DIM, 2 * GATE_LANES), lambda i, h: (0, h, 0, 0)),
            pl.BlockSpec((1, 2 * GATE_LANES), lambda i, h: (0, 0)),
            head_vec, head_vec,
        ],
        out_specs=[head_out] * 3 + [pl.BlockSpec((ROW_TILE, 2 * GATE_LANES), lambda i, h: (i, 0))]
        + [head_out] * (n_inner - 3),
        out_shape=[inner] * 3 + [jax.ShapeDtypeStruct((rows, 2 * GATE_LANES), F32)] + [inner] * (n_inner - 3),
        scratch_shapes=[pltpu.VMEM((ROW_TILE, D_MODEL), BF16)],
        compiler_params=_params("parallel", "arbitrary"),
        name="mlstm_proj",
    )(xs, mod, norm_g, w_in, w_in, w_in, conv_w, wq, wk, wv, wg, bg, m_norm_g, m_skip)
    return outs


def _log_sigmoid(x):
    return jnp.minimum(x, 0.0) - jnp.log1p(jnp.exp(-jnp.abs(x)))


def _split3(x):
    hi = x.astype(BF16)
    r = x - hi.astype(F32)
    mid = r.astype(BF16)
    lo = (r - mid.astype(F32)).astype(BF16)
    return hi, mid, lo


def _gate_prep_kernel(gcol_ref, grow_ref, bc_ref, gc_ref, cr_ref, blr_ref, *, nchunks):
    L = SCAN_CHUNK
    r = lax.broadcasted_iota(jnp.int32, (L, L), 0)
    c = lax.broadcasted_iota(jnp.int32, (L, L), 1)
    lower = (c <= r).astype(BF16)
    upper = (c >= r).astype(BF16)
    fwd_lane = lax.broadcasted_iota(jnp.int32, (1, GATE_LANES), 1) < HEADS
    fwd_row = lax.broadcasted_iota(jnp.int32, (2 * HEADS, 1), 0) < HEADS
    for ci in range(nchunks):
        rows = pl.ds(ci * L, L)
        ig = gcol_ref[rows, :GATE_LANES]
        parts = _split3(_log_sigmoid(gcol_ref[rows, GATE_LANES:]))
        pre = sum(_dot(lower, p) for p in parts)
        suf = sum(_dot(upper, p) for p in parts)
        bcum = jnp.where(fwd_lane, pre, suf)
        b_last = jnp.where(fwd_lane, pre[L - 1:L, :], suf[0:1, :])
        bc_ref[rows, :] = bcum
        gc_ref[rows, :] = b_last - bcum + ig
        ig_r = grow_ref[0:2 * HEADS, rows]
        parts_r = _split3(_log_sigmoid(grow_ref[:, rows]))
        pre_r = sum(_dot(p, upper) for p in parts_r)[2 * HEADS:, :]
        suf_r = sum(_dot(p, lower) for p in parts_r)[2 * HEADS:, :]
        cr_ref[ci] = ig_r - jnp.where(fwd_row, pre_r, suf_r)
        b_last_r = jnp.where(fwd_row, pre_r[:, L - 1:L], suf_r[:, 0:1])
        blr_ref[ci] = jnp.broadcast_to(b_last_r, (2 * HEADS, GATE_LANES))


def _gate_prep(gates_col, gates_row, tokens):
    nchunks = tokens // SCAN_CHUNK
    col = pl.BlockSpec((None, tokens, GATE_LANES), lambda b: (b, 0, 0))
    return pl.pallas_call(
        functools.partial(_gate_prep_kernel, nchunks=nchunks),
        grid=(BATCH,),
        in_specs=[
            pl.BlockSpec((None, tokens, 2 * GATE_LANES), lambda b: (b, 0, 0)),
            pl.BlockSpec((None, 4 * HEADS, tokens), lambda b: (b, 0, 0)),
        ],
        out_specs=[
            col, col,
            pl.BlockSpec((None, nchunks, 2 * HEADS, SCAN_CHUNK), lambda b: (b, 0, 0, 0)),
            pl.BlockSpec((None, nchunks, 2 * HEADS, GATE_LANES), lambda b: (b, 0, 0, 0)),
        ],
        out_shape=[
            jax.ShapeDtypeStruct((BATCH, tokens, GATE_LANES), F32),
            jax.ShapeDtypeStruct((BATCH, tokens, GATE_LANES), F32),
            jax.ShapeDtypeStruct((BATCH, nchunks, 2 * HEADS, SCAN_CHUNK), F32),
            jax.ShapeDtypeStruct((BATCH, nchunks, 2 * HEADS, GATE_LANES), F32),
        ],
        compiler_params=_params("parallel"),
        name="gate_prep",
    )(gates_col, gates_row)


class _Seq:
    def __init__(self, q, k, v, bc, gc, cr, blr, hs, hn, nchunks):
        self.q, self.k, self.v, self.bc, self.gc, self.cr, self.blr = q, k, v, bc, gc, cr, blr
        self.hs, self.hn, self.nchunks = hs, hn, nchunks


def _scan_chunk(seq, ci, direction, head, c_ref, n, m):
    L = SCAN_CHUNK
    rows = pl.ds(pl.multiple_of(ci * L, L), L)
    gate = direction * HEADS + head
    lane_sel = lax.broadcasted_iota(jnp.int32, (1, GATE_LANES), 1) == gate
    pick = lambda ref: jnp.sum(jnp.where(lane_sel, ref[rows, :], 0.0), axis=-1, keepdims=True)
    q = seq.q[rows, :]
    k = seq.k[rows, :]
    v = seq.v[rows, :]
    bcum = pick(seq.bc)
    g = pick(seq.gc)
    b_last = seq.blr[ci, pl.ds(gate, 1), :][:, 0:1]

    if seq.hn is not None:
        j = lax.broadcasted_iota(jnp.int32, (L, L), 0)
        s = lax.broadcasted_iota(jnp.int32, (L, L), 1)
        seen = (s <= j) if direction == 0 else (s >= j)
        dmat = jnp.where(seen, bcum + seq.cr[ci, pl.ds(gate, 1), :], -jnp.inf)
        m_inter = bcum + m
        m_j = jnp.maximum(m_inter, jnp.max(dmat, axis=-1, keepdims=True))
        sc = lax.dot_general(q, k, (((1,), (1,)), ((), ())), preferred_element_type=F32)
        sc = sc * jnp.exp(dmat - m_j)
        w_inter = jnp.exp(m_inter - m_j)
        num = w_inter * _dot(q, c_ref[...].astype(BF16)) + _dot(sc.astype(BF16), v)
        den = (w_inter * jnp.sum(q.astype(F32) * n, axis=-1, keepdims=True)
               + jnp.sum(sc, axis=-1, keepdims=True))
        h = num / jnp.maximum(jnp.abs(den), jnp.exp(-m_j))
        if direction == 0:
            seq.hs[rows, :] = h
        else:
            hsum = seq.hs[rows, :] + h
            hc = hsum - jnp.mean(hsum, axis=-1, keepdims=True)
            var = jnp.mean(hc * hc, axis=-1, keepdims=True)
            seq.hn[rows, :] = (hc * lax.rsqrt(var + EPS)).astype(BF16)

    m_new = jnp.maximum(b_last + m, jnp.max(g, axis=0, keepdims=True))
    w_decay = jnp.exp(b_last + m - m_new)
    kw = k.astype(F32) * jnp.exp(g - m_new)
    c_ref[...] = w_decay * c_ref[...] + lax.dot_general(
        kw.astype(BF16), v, (((0,), (0,)), ((), ())), preferred_element_type=F32)
    n_new = w_decay * n + jnp.sum(kw, axis=0, keepdims=True)
    return n_new, m_new


def _scan_kernel(*refs, ctx_out):
    (qx, kx, vx, bcx, gcx, crx, blrx, qc, kc, vc, bcc, gcc, crc, blrc) = refs[:14]
    if ctx_out:
        hnx, hnc, c_ref, hsx, hsc = refs[14:]
    else:
        hnx, c_ref, hsx = refs[14:]
        hnc = hsc = None
    head = pl.program_id(1)
    ctx = _Seq(qc, kc, vc, bcc, gcc, crc, blrc, hsc, hnc, CTX_LEN // SCAN_CHUNK)
    lat = _Seq(qx, kx, vx, bcx, gcx, crx, blrx, hsx, hnx, SEQ // SCAN_CHUNK)
    for direction in (0, 1):
        c_ref[...] = jnp.zeros_like(c_ref)
        state = (jnp.zeros((1, HEAD_DIM), F32), jnp.zeros((1, 1), F32))
        for seq in (ctx, lat):
            def body(i, st, seq=seq):
                ci = i if direction == 0 else seq.nchunks - 1 - i
                return _scan_chunk(seq, ci, direction, head, c_ref, *st)
            state = lax.fori_loop(0, seq.nchunks, body, state)


def _scan(lat, ctx, *, ctx_out):
    def specs(tokens):
        nch = tokens // SCAN_CHUNK
        qkv = pl.BlockSpec((None, tokens, HEAD_DIM), lambda b, h: (b, 0, h))
        col = pl.BlockSpec((None, tokens, GATE_LANES), lambda b, h: (b, 0, 0))
        return [qkv, qkv, qkv, col, col,
                pl.BlockSpec((None, nch, 2 * HEADS, SCAN_CHUNK), lambda b, h: (b, 0, 0, 0)),
                pl.BlockSpec((None, nch, 2 * HEADS, GATE_LANES), lambda b, h: (b, 0, 0, 0))]

    hn_spec = lambda tokens: pl.BlockSpec((None, tokens, HEAD_DIM), lambda b, h: (b, 0, h))
    hn_shape = lambda tokens: jax.ShapeDtypeStruct((BATCH, tokens, D_INNER), BF16)
    out_specs = [hn_spec(SEQ)] + ([hn_spec(CTX_LEN)] if ctx_out else [])
    out_shape = [hn_shape(SEQ)] + ([hn_shape(CTX_LEN)] if ctx_out else [])
    scratch = [pltpu.VMEM((HEAD_DIM, HEAD_DIM), F32), pltpu.VMEM((SEQ, HEAD_DIM), F32)]
    if ctx_out:
        scratch.append(pltpu.VMEM((CTX_LEN, HEAD_DIM), F32))
    outs = pl.pallas_call(
        functools.partial(_scan_kernel, ctx_out=ctx_out),
        grid=(BATCH, HEADS),
        in_specs=specs(SEQ) + specs(CTX_LEN),
        out_specs=out_specs,
        out_shape=out_shape,
        scratch_shapes=scratch,
        compiler_params=_params("parallel", "arbitrary"),
        name="mlstm_scan",
    )(*lat, *ctx)
    return outs if ctx_out else (outs[0], None)


def _mlstm_out_kernel(x_ref, mod_ref, hn_ref, a_ref, b_ref, wo_ref, fg_ref, out_ref, acc_ref, *, final):
    k = pl.program_id(1)

    @pl.when(k == 0)
    def _():
        acc_ref[...] = jnp.zeros_like(acc_ref)

    for rows in _sub_tiles():
        y = hn_ref[rows, :].astype(F32) * a_ref[rows, :].astype(F32) + b_ref[rows, :].astype(F32)
        acc_ref[rows, :] += _dot(y.astype(BF16), wo_ref[...])

    @pl.when(k == pl.num_programs(1) - 1)
    def _():
        for rows in _sub_tiles():
            xn = x_ref[rows, :] + mod_ref[:, 2 * D_MODEL:] * acc_ref[rows, :]
            if final:
                xn = xn * lax.rsqrt(jnp.mean(xn * xn, axis=-1, keepdims=True) + EPS) * fg_ref[...]
            out_ref[rows, :] = xn


def _mlstm_out(xs, mod, hn, a, b, w_out, final_g, *, tiles_per_batch, final):
    rows = xs.shape[0]
    nk = D_INNER // COL_BLOCK
    row_spec = pl.BlockSpec((ROW_TILE, D_MODEL), lambda i, k: (i, 0))
    inner = pl.BlockSpec((ROW_TILE, COL_BLOCK), lambda i, k: (i, k))
    return pl.pallas_call(
        functools.partial(_mlstm_out_kernel, final=final),
        grid=(rows // ROW_TILE, nk),
        in_specs=[
            row_spec,
            pl.BlockSpec((None, 1, 3 * D_MODEL), _mod_index(tiles_per_batch)),
            inner, inner, inner,
            pl.BlockSpec((COL_BLOCK, D_MODEL), lambda i, k: (k, 0)),
            pl.BlockSpec((1, D_MODEL), lambda i, k: (0, 0)),
        ],
        out_specs=row_spec,
        out_shape=jax.ShapeDtypeStruct((rows, D_MODEL), F32),
        scratch_shapes=[pltpu.VMEM((ROW_TILE, D_MODEL), F32)],
        compiler_params=_params("parallel", "arbitrary"),
        name="mlstm_out",
    )(xs, mod, hn, a, b, w_out, final_g)


def _gate_weights(w_gate, b_gate):
    def place(a):
        a = a.reshape(a.shape[:-1] + (4, HEADS))
        ig = a[..., 0::2, :].reshape(a.shape[:-2] + (2 * HEADS,))
        fg = a[..., 1::2, :].reshape(a.shape[:-2] + (2 * HEADS,))
        pad = [(0, 0)] * (ig.ndim - 1) + [(0, GATE_LANES - 2 * HEADS)]
        return jnp.concatenate([jnp.pad(ig, pad), jnp.pad(fg, pad)], axis=-1)

    wg = place(w_gate).reshape(3, HEADS, HEAD_DIM, 2 * GATE_LANES).astype(BF16)
    bg = place(b_gate).reshape(1, 2 * GATE_LANES)
    return wg, bg


def _gate_layouts(gates, tokens):
    col = gates.reshape(BATCH, tokens, 2 * GATE_LANES)
    used = jnp.concatenate([col[..., :2 * HEADS], col[..., GATE_LANES:GATE_LANES + 2 * HEADS]], axis=-1)
    return col, jnp.swapaxes(used, 1, 2)


def kernel(x, c, ctx, c_ctx, norm_g, mod_w, mod_b, conv_w_in, conv_w, conv_w_out, m_w_in, m_conv_w, m_wq,
           m_wk, m_wv, m_w_gate, m_b_gate, m_norm_g, m_skip, m_w_out, final_g):
    assert x.shape == (BATCH, SEQ, D_MODEL) and ctx.shape == (BATCH, CTX_LEN, D_MODEL)
    c_all = jnp.zeros((MOD_ROWS, D_MODEL), F32).at[:BATCH].set(c).at[CTX_MOD_ROW].set(c_ctx)
    mods = _modulation(c_all, mod_w, mod_b).reshape(DEPTH, MOD_ROWS, 1, 3 * D_MODEL)
    xs = x.reshape(BATCH * SEQ, D_MODEL)
    cs = ctx.reshape(BATCH * CTX_LEN, D_MODEL)
    lat_tiles = SEQ // ROW_TILE
    final_g2 = final_g.reshape(1, D_MODEL)
    for i in range(DEPTH):
        last = i == DEPTH - 1
        j = i // 2
        g = norm_g[i].reshape(1, D_MODEL)
        if i % 2 == 0:
            w_in, w_out = conv_w_in[j].astype(BF16), conv_w_out[j].astype(BF16)
            xs = _conv_layer(xs, mods[i], g, w_in, conv_w[j], w_out, period=GRID_W, tiles_per_batch=lat_tiles)
            if not last:
                cs = _conv_layer(cs, mods[i], g, w_in, conv_w[j], w_out, period=CTX_LEN, tiles_per_batch=None)
        else:
            w_in, w_out = m_w_in[j].astype(BF16), m_w_out[j].astype(BF16)
            wq, wk, wv = m_wq[j].astype(BF16), m_wk[j].astype(BF16), m_wv[j].astype(BF16)
            wg, bg = _gate_weights(m_w_gate[j], m_b_gate[j])
            proj = functools.partial(_mlstm_proj, norm_g=g, w_in=w_in, conv_w=m_conv_w[j], wq=wq, wk=wk, wv=wv,
                                     wg=wg, bg=bg, m_norm_g=m_norm_g[j].reshape(1, D_INNER),
                                     m_skip=m_skip[j].reshape(1, D_INNER))
            px = proj(xs, mods[i], period=GRID_W, tiles_per_batch=lat_tiles, gated=True)
            pc = proj(cs, mods[i], period=CTX_LEN, tiles_per_batch=None, gated=not last)

            def stream(p, tokens):
                qkv = [a.reshape(BATCH, tokens, D_INNER) for a in p[:3]]
                return qkv + list(_gate_prep(*_gate_layouts(p[3], tokens), tokens))

            hnx, hnc = _scan(stream(px, SEQ), stream(pc, CTX_LEN), ctx_out=not last)
            out = functools.partial(_mlstm_out, w_out=w_out, final_g=final_g2)
            if not last:
                cs = out(cs, mods[i], hnc.reshape(-1, D_INNER), *pc[4:], tiles_per_batch=None, final=False)
            xs = out(xs, mods[i], hnx.reshape(-1, D_INNER), *px[4:], tiles_per_batch=lat_tiles, final=last)
    return xs.reshape(BATCH, SEQ, D_MODEL)
```

```python
import functools

import jax
import jax.numpy as jnp
from jax import lax
from jax.experimental import pallas as pl
from jax.experimental.pallas import tpu as pltpu

D_MODEL = 1024
BATCH = 16
SEQ = 2048
DEPTH = 4
GRID_W = 64
CTX_LEN = 256
D_INNER = 2 * D_MODEL
HEADS = 4
HEAD_DIM = D_INNER // HEADS
EPS = 1e-6

SCAN_CHUNK = 256
COL_BLOCK = 512
ROW_TILE = 1024
SUB_ROWS = 512
MOD_ROWS = 24
CTX_MOD_ROW = BATCH
GATE_LANES = 128
VMEM_LIMIT_BYTES = 56 * 1024 * 1024

F32 = jnp.float32
BF16 = jnp.bfloat16


def _dot(a, b):
    return jnp.dot(a, b, preferred_element_type=F32)


def _sigmoid(x):
    return 0.5 * jnp.tanh(0.5 * x) + 0.5


def _silu(x):
    return x * _sigmoid(x)


def _params(*sem):
    return pltpu.CompilerParams(dimension_semantics=sem, vmem_limit_bytes=VMEM_LIMIT_BYTES)


def _mod_kernel(c_ref, w_ref, b_ref, o_ref):
    sc = _silu(c_ref[...])
    o_ref[...] = jnp.dot(sc, w_ref[...], preferred_element_type=F32,
                         precision=lax.Precision.HIGHEST) + b_ref[...]


def _modulation(c_all, mod_w, mod_b):
    nblk = 3 * D_MODEL // D_MODEL
    return pl.pallas_call(
        _mod_kernel,
        grid=(DEPTH, nblk),
        in_specs=[
            pl.BlockSpec((MOD_ROWS, D_MODEL), lambda l, j: (0, 0)),
            pl.BlockSpec((None, D_MODEL, D_MODEL), lambda l, j: (l, 0, j)),
            pl.BlockSpec((None, 1, D_MODEL), lambda l, j: (l, 0, j)),
        ],
        out_specs=pl.BlockSpec((None, MOD_ROWS, D_MODEL), lambda l, j: (l, 0, j)),
        out_shape=jax.ShapeDtypeStruct((DEPTH, MOD_ROWS, 3 * D_MODEL), F32),
        compiler_params=_params("arbitrary", "arbitrary"),
        name="modulation",
    )(c_all, mod_w, mod_b.reshape(DEPTH, 1, 3 * D_MODEL))


def _modulated_norm(x, g, mod):
    y = x * lax.rsqrt(jnp.mean(x * x, axis=-1, keepdims=True) + EPS) * g
    return y * (1.0 + mod[:, D_MODEL:2 * D_MODEL]) + mod[:, :D_MODEL]


def _conv3(u, w, period):
    rows = u.shape[0]
    t = lax.broadcasted_iota(jnp.int32, (rows, 1), 0) & (period - 1)
    prev = jnp.where(t != 0, pltpu.roll(u, 1, axis=0), 0.0)
    nxt = jnp.where(t != period - 1, pltpu.roll(u, rows - 1, axis=0), 0.0)
    return prev * w[0:1, :] + u * w[1:2, :] + nxt * w[2:3, :]


def _sub_tiles():
    return [pl.ds(r * SUB_ROWS, SUB_ROWS) for r in range(ROW_TILE // SUB_ROWS)]


def _mod_index(tiles_per_batch):
    if tiles_per_batch is None:
        return lambda i, k: (CTX_MOD_ROW, 0, 0)
    return lambda i, k: (i // tiles_per_batch, 0, 0)


def _conv_layer_kernel(x_ref, mod_ref, g_ref, wb_ref, wc_ref, wu_ref, wz_ref, cw_ref, wo_ref,
                       o_ref, hx_ref, acc_ref, *, period):
    k = pl.program_id(1)

    @pl.when(k == 0)
    def _():
        for rows in _sub_tiles():
            hx_ref[rows, :] = _modulated_norm(x_ref[rows, :], g_ref[...], mod_ref[...]).astype(BF16)
        acc_ref[...] = jnp.zeros_like(acc_ref)

    for rows in _sub_tiles():
        hx = hx_ref[rows, :]
        cu = _dot(hx, wc_ref[...]) * _dot(hx, wu_ref[...])
        y = _dot(hx, wb_ref[...]) * _conv3(cu, cw_ref[...], period)
        y = y * _silu(_dot(hx, wz_ref[...]))
        acc_ref[rows, :] += _dot(y.astype(BF16), wo_ref[...])

    @pl.when(k == pl.num_programs(1) - 1)
    def _():
        for rows in _sub_tiles():
            o_ref[rows, :] = x_ref[rows, :] + mod_ref[:, 2 * D_MODEL:] * acc_ref[rows, :]


def _conv_layer(xs, mod, norm_g, w_in, conv_w, w_out, *, period, tiles_per_batch):
    rows = xs.shape[0]
    nk = D_INNER // COL_BLOCK
    row_spec = pl.BlockSpec((ROW_TILE, D_MODEL), lambda i, k: (i, 0))
    w_in_spec = lambda g: pl.BlockSpec((D_MODEL, COL_BLOCK), lambda i, k: (0, g * nk + k))
    return pl.pallas_call(
        functools.partial(_conv_layer_kernel, period=period),
        grid=(rows // ROW_TILE, nk),
        in_specs=[
            row_spec,
            pl.BlockSpec((None, 1, 3 * D_MODEL), _mod_index(tiles_per_batch)),
            pl.BlockSpec((1, D_MODEL), lambda i, k: (0, 0)),
            w_in_spec(0), w_in_spec(1), w_in_spec(2), w_in_spec(3),
            pl.BlockSpec((3, COL_BLOCK), lambda i, k: (0, k)),
            pl.BlockSpec((COL_BLOCK, D_MODEL), lambda i, k: (k, 0)),
        ],
        out_specs=row_spec,
        out_shape=jax.ShapeDtypeStruct((rows, D_MODEL), F32),
        scratch_shapes=[pltpu.VMEM((ROW_TILE, D_MODEL), BF16), pltpu.VMEM((ROW_TILE, D_MODEL), F32)],
        compiler_params=_params("parallel", "arbitrary"),
        name="conv_layer",
    )(xs, mod, norm_g, w_in, w_in, w_in, w_in, conv_w, w_out)


def _mlstm_proj_kernel(x_ref, mod_ref, g_ref, wu_ref, wz_ref, wo_ref, cw_ref, wq_ref, wk_ref, wv_ref,
                       wg_ref, bg_ref, ng_ref, sk_ref, qt_ref, k_ref, vt_ref, gates_ref, *rest, period, gated):
    hx_ref = rest[-1]
    h = pl.program_id(1)

    @pl.when(h == 0)
    def _():
        for rows in _sub_tiles():
            hx_ref[rows, :] = _modulated_norm(x_ref[rows, :], g_ref[...], mod_ref[...]).astype(BF16)
        gates_ref[...] = jnp.broadcast_to(bg_ref[...], gates_ref.shape)

    for rows in _sub_tiles():
        hx = hx_ref[rows, :]
        u = _dot(hx, wu_ref[...])
        uc = _silu(_conv3(u, cw_ref[...], period))
        ucb = uc.astype(BF16)
        q = _dot(ucb, wq_ref[...])
        v = _dot(u.astype(BF16), wv_ref[...])
        qb = q.astype(BF16)
        kb = (_dot(ucb, wk_ref[...]) * (HEAD_DIM ** -0.5)).astype(BF16)
        vb = v.astype(BF16)
        k_ref[rows, :] = kb
        for piece in range(SUB_ROWS // SCAN_CHUNK):
            chunk = rows.start // SCAN_CHUNK + piece
            part = slice(piece * SCAN_CHUNK, (piece + 1) * SCAN_CHUNK)
            qt_ref[chunk] = q[part, :].T.astype(BF16)
            vt_ref[chunk] = v[part, :].T.astype(BF16)
        gates_ref[rows, :] += _dot(qb, wg_ref[0]) + _dot(kb, wg_ref[1]) + _dot(vb, wg_ref[2])
        if gated:
            a_ref, b_ref = rest[:2]
            sz = _silu(_dot(hx, wz_ref[...]))
            a_ref[rows, :] = (_sigmoid(_dot(hx, wo_ref[...])) * ng_ref[...] * sz).astype(BF16)
            b_ref[rows, :] = (sk_ref[...] * uc * sz).astype(BF16)


def _mlstm_proj(xs, mod, norm_g, w_in, conv_w, wq, wk, wv, wg, bg, m_norm_g, m_skip, *, period,
                tiles_per_batch, gated):
    rows = xs.shape[0]
    w_in_spec = lambda g: pl.BlockSpec((D_MODEL, HEAD_DIM), lambda i, h: (0, g * HEADS + h))
    head_w = pl.BlockSpec((None, HEAD_DIM, HEAD_DIM), lambda i, h: (h, 0, 0))
    head_out = pl.BlockSpec((ROW_TILE, HEAD_DIM), lambda i, h: (i, h))
    head_vec = pl.BlockSpec((1, HEAD_DIM), lambda i, h: (0, h))
    inner = jax.ShapeDtypeStruct((rows, D_INNER), BF16)
    head_t = pl.BlockSpec((ROW_TILE // SCAN_CHUNK, HEAD_DIM, SCAN_CHUNK), lambda i, h: (i, h, 0))
    inner_t = jax.ShapeDtypeStruct((rows // SCAN_CHUNK, D_INNER, SCAN_CHUNK), BF16)
    n_inner = 5 if gated else 3
    outs = pl.pallas_call(
        functools.partial(_mlstm_proj_kernel, period=period, gated=gated),
        grid=(rows // ROW_TILE, HEADS),
        in_specs=[
            pl.BlockSpec((ROW_TILE, D_MODEL), lambda i, h: (i, 0)),
            pl.BlockSpec((None, 1, 3 * D_MODEL), _mod_index(tiles_per_batch)),
            pl.BlockSpec((1, D_MODEL), lambda i, h: (0, 0)),
            w_in_spec(0), w_in_spec(1), w_in_spec(2),
            pl.BlockSpec((3, HEAD_DIM), lambda i, h: (0, h)),
            head_w, head_w, head_w,
            pl.BlockSpec((3, None, HEAD_DIM, 2 * GATE_LANES), lambda i, h: (0, h, 0, 0)),
            pl.BlockSpec((1, 2 * GATE_LANES), lambda i, h: (0, 0)),
            head_vec, head_vec,
        ],
        out_specs=[head_t, head_out, head_t, pl.BlockSpec((ROW_TILE, 2 * GATE_LANES), lambda i, h: (i, 0))]
        + [head_out] * (n_inner - 3),
        out_shape=[inner_t, inner, inner_t, jax.ShapeDtypeStruct((rows, 2 * GATE_LANES), F32)]
        + [inner] * (n_inner - 3),
        scratch_shapes=[pltpu.VMEM((ROW_TILE, D_MODEL), BF16)],
        compiler_params=_params("parallel", "arbitrary"),
        name="mlstm_proj",
    )(xs, mod, norm_g, w_in, w_in, w_in, conv_w, wq, wk, wv, wg, bg, m_norm_g, m_skip)
    return outs


def _log_sigmoid(x):
    return jnp.minimum(x, 0.0) - jnp.log1p(jnp.exp(-jnp.abs(x)))


def _split3(x):
    hi = x.astype(BF16)
    r = x - hi.astype(F32)
    mid = r.astype(BF16)
    lo = (r - mid.astype(F32)).astype(BF16)
    return hi, mid, lo


def _running_max_lanes(x, reverse):
    n = x.shape[-1]
    lane = lax.broadcasted_iota(jnp.int32, x.shape, 1)
    shift = 1
    while shift < n:
        if reverse:
            moved = jnp.where(lane < n - shift, pltpu.roll(x, n - shift, axis=1), -jnp.inf)
        else:
            moved = jnp.where(lane >= shift, pltpu.roll(x, shift, axis=1), -jnp.inf)
        x = jnp.maximum(x, moved)
        shift *= 2
    return x


ROW_CR, ROW_CM, ROW_BCUM, ROW_BLAST = 0, 1, 2, 3
ROW_KINDS = 4


def _gate_prep_kernel(gcol_ref, grow_ref, crc_ref, row_ref, *, nchunks):
    L = SCAN_CHUNK
    G = 2 * HEADS
    r = lax.broadcasted_iota(jnp.int32, (L, L), 0)
    c = lax.broadcasted_iota(jnp.int32, (L, L), 1)
    lower = (c <= r).astype(BF16)
    upper = (c >= r).astype(BF16)
    fwd_lane = lax.broadcasted_iota(jnp.int32, (1, GATE_LANES), 1) < HEADS
    fwd_row = lax.broadcasted_iota(jnp.int32, (G, 1), 0) < HEADS
    for ci in range(nchunks):
        rows = pl.ds(ci * L, L)
        parts = _split3(_log_sigmoid(gcol_ref[rows, GATE_LANES:]))
        pre = sum(_dot(lower, p) for p in parts)
        suf = sum(_dot(upper, p) for p in parts)
        crc_ref[rows, :] = gcol_ref[rows, :GATE_LANES] - jnp.where(fwd_lane, pre, suf)
        parts_r = _split3(_log_sigmoid(grow_ref[:, rows]))
        pre_r = sum(_dot(p, upper) for p in parts_r)[G:, :]
        suf_r = sum(_dot(p, lower) for p in parts_r)[G:, :]
        bcum_r = jnp.where(fwd_row, pre_r, suf_r)
        cr_r = grow_ref[0:G, rows] - bcum_r
        b_last_r = jnp.where(fwd_row, pre_r[:, L - 1:L], suf_r[:, 0:1])
        row_ref[ci, ROW_CR * G:(ROW_CR + 1) * G, :] = cr_r
        row_ref[ci, ROW_CM * G:(ROW_CM + 1) * G, :] = jnp.where(
            fwd_row, _running_max_lanes(cr_r, False), _running_max_lanes(cr_r, True))
        row_ref[ci, ROW_BCUM * G:(ROW_BCUM + 1) * G, :] = bcum_r
        row_ref[ci, ROW_BLAST * G:(ROW_BLAST + 1) * G, :] = jnp.broadcast_to(b_last_r, (G, L))


def _gate_prep(gates_col, gates_row, tokens):
    nchunks = tokens // SCAN_CHUNK
    row_rows = ROW_KINDS * 2 * HEADS
    return pl.pallas_call(
        functools.partial(_gate_prep_kernel, nchunks=nchunks),
        grid=(BATCH,),
        in_specs=[
            pl.BlockSpec((None, tokens, 2 * GATE_LANES), lambda b: (b, 0, 0)),
            pl.BlockSpec((None, 4 * HEADS, tokens), lambda b: (b, 0, 0)),
        ],
        out_specs=[
            pl.BlockSpec((None, tokens, GATE_LANES), lambda b: (b, 0, 0)),
            pl.BlockSpec((None, nchunks, row_rows, SCAN_CHUNK), lambda b: (b, 0, 0, 0)),
        ],
        out_shape=[
            jax.ShapeDtypeStruct((BATCH, tokens, GATE_LANES), F32),
            jax.ShapeDtypeStruct((BATCH, nchunks, row_rows, SCAN_CHUNK), F32),
        ],
        compiler_params=_params("parallel"),
        name="gate_prep",
    )(gates_col, gates_row)


class _Seq:
    def __init__(self, k, qt, vt, crc, row, hs, hn, nchunks):
        self.k, self.qt, self.vt, self.crc, self.row = k, qt, vt, crc, row
        self.hs, self.hn, self.nchunks = hs, hn, nchunks


def _chunk_rows(ci):
    start = ci * SCAN_CHUNK
    return pl.ds(start if isinstance(ci, int) else pl.multiple_of(start, SCAN_CHUNK), SCAN_CHUNK)


def _dot_row_halves(a, b, a2=None, b2=None):
    rows = a.shape[0]
    cut = (rows // 32) * 16
    parts = []
    for part in (slice(0, cut), slice(cut, rows)):
        acc = _dot(a[part], b)
        if a2 is not None:
            acc = acc + _dot(a2[part], b2)
        parts.append(acc)
    return jnp.concatenate(parts, axis=0)


def _direction_step(seq, ci, direction, head, ct_ref, n, m):
    L = SCAN_CHUNK
    rows = _chunk_rows(ci)
    gate = direction * HEADS + head
    row = lambda kind: seq.row[ci, pl.ds(kind * 2 * HEADS + gate, 1), :]
    k = seq.k[rows, :]
    qt = seq.qt[ci]
    vt = seq.vt[ci]
    cr_row, cm_row = row(ROW_CR), row(ROW_CM)
    top_j = jnp.maximum(m, cm_row)
    want_h = seq.hn is not None
    ct = ct_ref[...]

    if want_h:
        n_rows = jnp.broadcast_to(n.astype(BF16), (16, HEAD_DIM))
        kq = _dot_row_halves(jnp.concatenate([k, n_rows], axis=0), qt)

    top = jnp.maximum(m, jnp.max(cr_row, axis=-1, keepdims=True))
    w_decay = jnp.exp(m - top)
    ws = jnp.exp(cr_row - top).astype(BF16)
    upd = _dot(jnp.concatenate([vt * ws, jnp.broadcast_to(ws, (16, L))], axis=0), k)

    h = None
    if want_h:
        lane_sel = lax.broadcasted_iota(jnp.int32, (1, GATE_LANES), 1) == gate
        cr_col = jnp.sum(jnp.where(lane_sel, seq.crc[rows, :], 0.0), axis=-1, keepdims=True)
        s = lax.broadcasted_iota(jnp.int32, (L, L), 0)
        j = lax.broadcasted_iota(jnp.int32, (L, L), 1)
        seen = (s <= j) if direction == 0 else (s >= j)
        decay = jnp.exp(jnp.where(seen, cr_col - cm_row, -jnp.inf))
        sc = kq[:L] * decay
        w_intra = jnp.exp(cm_row - top_j)
        w_inter = jnp.exp(m - top_j)
        den = w_inter * kq[L:L + 1] + w_intra * jnp.sum(sc, axis=0, keepdims=True)
        inv = 1.0 / jnp.maximum(jnp.abs(den), jnp.exp(-(row(ROW_BCUM) + top_j)))
        qs = qt * (w_inter * inv).astype(BF16)
        ss = (sc * (w_intra * inv)).astype(BF16)
        h = _dot_row_halves(ct.astype(BF16), qs, vt, ss)

    ct_ref[...] = w_decay * ct + upd[:HEAD_DIM]
    n_new = w_decay * n + upd[HEAD_DIM:HEAD_DIM + 1]
    m_new = row(ROW_BLAST)[:, 0:1] + top
    return h, n_new, m_new


def _head_norm_t(hsum):
    hc = hsum - jnp.mean(hsum, axis=0, keepdims=True)
    var = jnp.mean(hc * hc, axis=0, keepdims=True)
    return (hc * lax.rsqrt(var + EPS)).T.astype(BF16)


def _scan_stream(seq, direction, head, ct_ref, state):
    nch = seq.nchunks

    def body(i, st):
        ci = i if direction == 0 else nch - 1 - i
        h, n, m = _direction_step(seq, ci, direction, head, ct_ref, *st)
        if seq.hn is not None:
            if direction == 0:
                seq.hs[ci] = h
            else:
                seq.hn[_chunk_rows(ci), :] = _head_norm_t(seq.hs[ci] + h)
        return n, m

    if nch == 1:
        return body(0, state)
    return lax.fori_loop(0, nch, body, state, unroll=4)


def _scan_kernel(*refs, ctx_out):
    kx, qtx, vtx, crcx, rowx, kc, qtc, vtc, crcc, rowc = refs[:10]
    refs = list(refs[10:])
    hnx = refs.pop(0)
    hnc = refs.pop(0) if ctx_out else None
    ct_ref, hsx = refs[:2]
    hsc = refs[2] if ctx_out else None
    head = pl.program_id(1)
    ctx = _Seq(kc, qtc, vtc, crcc, rowc, hsc, hnc, CTX_LEN // SCAN_CHUNK)
    lat = _Seq(kx, qtx, vtx, crcx, rowx, hsx, hnx, SEQ // SCAN_CHUNK)
    for direction in (0, 1):
        ct_ref[...] = jnp.zeros_like(ct_ref)
        state = (jnp.zeros((1, HEAD_DIM), F32), jnp.zeros((1, 1), F32))
        state = _scan_stream(ctx, direction, head, ct_ref, state)
        _scan_stream(lat, direction, head, ct_ref, state)


def _scan(lat, ctx, *, ctx_out):
    def specs(tokens):
        nch = tokens // SCAN_CHUNK
        qv_t = pl.BlockSpec((None, nch, HEAD_DIM, SCAN_CHUNK), lambda b, h: (b, 0, h, 0))
        return [pl.BlockSpec((None, tokens, HEAD_DIM), lambda b, h: (b, 0, h)), qv_t, qv_t,
                pl.BlockSpec((None, tokens, GATE_LANES), lambda b, h: (b, 0, 0)),
                pl.BlockSpec((None, nch, ROW_KINDS * 2 * HEADS, SCAN_CHUNK), lambda b, h: (b, 0, 0, 0))]

    hn_spec = lambda tokens: pl.BlockSpec((None, tokens, HEAD_DIM), lambda b, h: (b, 0, h))
    hn_shape = lambda tokens: jax.ShapeDtypeStruct((BATCH, tokens, D_INNER), BF16)
    hs_shape = lambda tokens: pltpu.VMEM((tokens // SCAN_CHUNK, HEAD_DIM, SCAN_CHUNK), F32)
    out_specs = [hn_spec(SEQ)] + ([hn_spec(CTX_LEN)] if ctx_out else [])
    out_shape = [hn_shape(SEQ)] + ([hn_shape(CTX_LEN)] if ctx_out else [])
    scratch = [pltpu.VMEM((HEAD_DIM, HEAD_DIM), F32), hs_shape(SEQ)] + ([hs_shape(CTX_LEN)] if ctx_out else [])
    outs = pl.pallas_call(
        functools.partial(_scan_kernel, ctx_out=ctx_out),
        grid=(BATCH, HEADS),
        in_specs=specs(SEQ) + specs(CTX_LEN),
        out_specs=out_specs,
        out_shape=out_shape,
        scratch_shapes=scratch,
        compiler_params=_params("parallel", "arbitrary"),
        name="mlstm_scan",
    )(*lat, *ctx)
    return outs if ctx_out else (outs[0], None)


def _mlstm_out_kernel(x_ref, mod_ref, hn_ref, a_ref, b_ref, wo_ref, fg_ref, out_ref, acc_ref, *, final):
    k = pl.program_id(1)

    @pl.when(k == 0)
    def _():
        acc_ref[...] = jnp.zeros_like(acc_ref)

    for rows in _sub_tiles():
        y = hn_ref[rows, :].astype(F32) * a_ref[rows, :].astype(F32) + b_ref[rows, :].astype(F32)
        acc_ref[rows, :] += _dot(y.astype(BF16), wo_ref[...])

    @pl.when(k == pl.num_programs(1) - 1)
    def _():
        for rows in _sub_tiles():
            xn = x_ref[rows, :] + mod_ref[:, 2 * D_MODEL:] * acc_ref[rows, :]
            if final:
                xn = xn * lax.rsqrt(jnp.mean(xn * xn, axis=-1, keepdims=True) + EPS) * fg_ref[...]
            out_ref[rows, :] = xn


def _mlstm_out(xs, mod, hn, a, b, w_out, final_g, *, tiles_per_batch, final):
    rows = xs.shape[0]
    nk = D_INNER // COL_BLOCK
    row_spec = pl.BlockSpec((ROW_TILE, D_MODEL), lambda i, k: (i, 0))
    inner = pl.BlockSpec((ROW_TILE, COL_BLOCK), lambda i, k: (i, k))
    return pl.pallas_call(
        functools.partial(_mlstm_out_kernel, final=final),
        grid=(rows // ROW_TILE, nk),
        in_specs=[
            row_spec,
            pl.BlockSpec((None, 1, 3 * D_MODEL), _mod_index(tiles_per_batch)),
            inner, inner, inner,
            pl.BlockSpec((COL_BLOCK, D_MODEL), lambda i, k: (k, 0)),
            pl.BlockSpec((1, D_MODEL), lambda i, k: (0, 0)),
        ],
        out_specs=row_spec,
        out_shape=jax.ShapeDtypeStruct((rows, D_MODEL), F32),
        scratch_shapes=[pltpu.VMEM((ROW_TILE, D_MODEL), F32)],
        compiler_params=_params("parallel", "arbitrary"),
        name="mlstm_out",
    )(xs, mod, hn, a, b, w_out, final_g)


def _gate_weights(w_gate, b_gate):
    def place(a):
        a = a.reshape(a.shape[:-1] + (4, HEADS))
        ig = a[..., 0::2, :].reshape(a.shape[:-2] + (2 * HEADS,))
        fg = a[..., 1::2, :].reshape(a.shape[:-2] + (2 * HEADS,))
        pad = [(0, 0)] * (ig.ndim - 1) + [(0, GATE_LANES - 2 * HEADS)]
        return jnp.concatenate([jnp.pad(ig, pad), jnp.pad(fg, pad)], axis=-1)

    wg = place(w_gate).reshape(3, HEADS, HEAD_DIM, 2 * GATE_LANES).astype(BF16)
    bg = place(b_gate).reshape(1, 2 * GATE_LANES)
    return wg, bg


def _gate_layouts(gates, tokens):
    col = gates.reshape(BATCH, tokens, 2 * GATE_LANES)
    used = jnp.concatenate([col[..., :2 * HEADS], col[..., GATE_LANES:GATE_LANES + 2 * HEADS]], axis=-1)
    return col, jnp.swapaxes(used, 1, 2)


def kernel(x, c, ctx, c_ctx, norm_g, mod_w, mod_b, conv_w_in, conv_w, conv_w_out, m_w_in, m_conv_w, m_wq,
           m_wk, m_wv, m_w_gate, m_b_gate, m_norm_g, m_skip, m_w_out, final_g):
    assert x.shape == (BATCH, SEQ, D_MODEL) and ctx.shape == (BATCH, CTX_LEN, D_MODEL)
    c_all = jnp.zeros((MOD_ROWS, D_MODEL), F32).at[:BATCH].set(c).at[CTX_MOD_ROW].set(c_ctx)
    mods = _modulation(c_all, mod_w, mod_b).reshape(DEPTH, MOD_ROWS, 1, 3 * D_MODEL)
    xs = x.reshape(BATCH * SEQ, D_MODEL)
    cs = ctx.reshape(BATCH * CTX_LEN, D_MODEL)
    lat_tiles = SEQ // ROW_TILE
    final_g2 = final_g.reshape(1, D_MODEL)
    for i in range(DEPTH):
        last = i == DEPTH - 1
        j = i // 2
        g = norm_g[i].reshape(1, D_MODEL)
        if i % 2 == 0:
            w_in, w_out = conv_w_in[j].astype(BF16), conv_w_out[j].astype(BF16)
            xs = _conv_layer(xs, mods[i], g, w_in, conv_w[j], w_out, period=GRID_W, tiles_per_batch=lat_tiles)
            if not last:
                cs = _conv_layer(cs, mods[i], g, w_in, conv_w[j], w_out, period=CTX_LEN, tiles_per_batch=None)
        else:
            w_in, w_out = m_w_in[j].astype(BF16), m_w_out[j].astype(BF16)
            wq, wk, wv = m_wq[j].astype(BF16), m_wk[j].astype(BF16), m_wv[j].astype(BF16)
            wg, bg = _gate_weights(m_w_gate[j], m_b_gate[j])
            proj = functools.partial(_mlstm_proj, norm_g=g, w_in=w_in, conv_w=m_conv_w[j], wq=wq, wk=wk, wv=wv,
                                     wg=wg, bg=bg, m_norm_g=m_norm_g[j].reshape(1, D_INNER),
                                     m_skip=m_skip[j].reshape(1, D_INNER))
            px = proj(xs, mods[i], period=GRID_W, tiles_per_batch=lat_tiles, gated=True)
            pc = proj(cs, mods[i], period=CTX_LEN, tiles_per_batch=None, gated=not last)

            def stream(p, tokens):
                qt, k, vt, gates = p[:4]
                t_shape = (BATCH, tokens // SCAN_CHUNK, D_INNER, SCAN_CHUNK)
                return [k.reshape(BATCH, tokens, D_INNER), qt.reshape(t_shape), vt.reshape(t_shape)] + list(
                    _gate_prep(*_gate_layouts(gates, tokens), tokens))

            hnx, hnc = _scan(stream(px, SEQ), stream(pc, CTX_LEN), ctx_out=not last)
            out = functools.partial(_mlstm_out, w_out=w_out, final_g=final_g2)
            if not last:
                cs = out(cs, mods[i], hnc.reshape(-1, D_INNER), *pc[4:], tiles_per_batch=None, final=False)
            xs = out(xs, mods[i], hnx.reshape(-1, D_INNER), *px[4:], tiles_per_batch=lat_tiles, final=last)
    return xs.reshape(BATCH, SEQ, D_MODEL)
```

```python
import functools

import jax
import jax.numpy as jnp
from jax import lax
from jax.experimental import pallas as pl
from jax.experimental.pallas import tpu as pltpu

D_MODEL = 1024
BATCH = 16
SEQ = 2048
DEPTH = 4
GRID_W = 64
CTX_LEN = 256
D_INNER = 2 * D_MODEL
HEADS = 4
HEAD_DIM = D_INNER // HEADS
EPS = 1e-6

SCAN_CHUNK = 256
COL_BLOCK = 512
ROW_TILE = 1024
SUB_ROWS = 512
OUT_ROWS = 512
OUT_SUB_ROWS = 256
MOD_ROWS = 24
CTX_MOD_ROW = BATCH
GATE_LANES = 128
VMEM_LIMIT_BYTES = 56 * 1024 * 1024

F32 = jnp.float32
BF16 = jnp.bfloat16


def _dot(a, b):
    return jnp.dot(a, b, preferred_element_type=F32)


def _sigmoid(x):
    return 0.5 * jnp.tanh(0.5 * x) + 0.5


def _silu(x):
    return x * _sigmoid(x)


def _params(*sem):
    return pltpu.CompilerParams(dimension_semantics=sem, vmem_limit_bytes=VMEM_LIMIT_BYTES)


def _mod_kernel(c_ref, w_ref, b_ref, o_ref):
    sc = _silu(c_ref[...])
    o_ref[...] = jnp.dot(sc, w_ref[...], preferred_element_type=F32,
                         precision=lax.Precision.HIGHEST) + b_ref[...]


def _modulation(c_all, mod_w, mod_b):
    nblk = 3 * D_MODEL // D_MODEL
    return pl.pallas_call(
        _mod_kernel,
        grid=(DEPTH, nblk),
        in_specs=[
            pl.BlockSpec((MOD_ROWS, D_MODEL), lambda l, j: (0, 0)),
            pl.BlockSpec((None, D_MODEL, D_MODEL), lambda l, j: (l, 0, j)),
            pl.BlockSpec((None, 1, D_MODEL), lambda l, j: (l, 0, j)),
        ],
        out_specs=pl.BlockSpec((None, MOD_ROWS, D_MODEL), lambda l, j: (l, 0, j)),
        out_shape=jax.ShapeDtypeStruct((DEPTH, MOD_ROWS, 3 * D_MODEL), F32),
        compiler_params=_params("arbitrary", "arbitrary"),
        name="modulation",
    )(c_all, mod_w, mod_b.reshape(DEPTH, 1, 3 * D_MODEL))


def _modulated_norm(x, g, mod):
    y = x * lax.rsqrt(jnp.mean(x * x, axis=-1, keepdims=True) + EPS) * g
    return y * (1.0 + mod[:, D_MODEL:2 * D_MODEL]) + mod[:, :D_MODEL]


def _conv3(u, w, period):
    rows = u.shape[0]
    t = lax.broadcasted_iota(jnp.int32, (rows, 1), 0) & (period - 1)
    prev = jnp.where(t != 0, pltpu.roll(u, 1, axis=0), 0.0)
    nxt = jnp.where(t != period - 1, pltpu.roll(u, rows - 1, axis=0), 0.0)
    return prev * w[0:1, :] + u * w[1:2, :] + nxt * w[2:3, :]


def _sub_tiles():
    return [pl.ds(r * SUB_ROWS, SUB_ROWS) for r in range(ROW_TILE // SUB_ROWS)]


def _mod_index(tiles_per_batch):
    if tiles_per_batch is None:
        return lambda i, *_: (CTX_MOD_ROW, 0, 0)
    return lambda i, *_: (i // tiles_per_batch, 0, 0)


def _conv_layer_kernel(x_ref, mod_ref, g_ref, wb_ref, wc_ref, wu_ref, wz_ref, cw_ref, wo_ref,
                       o_ref, hx_ref, acc_ref, *, period):
    k = pl.program_id(1)

    @pl.when(k == 0)
    def _():
        for rows in _sub_tiles():
            hx_ref[rows, :] = _modulated_norm(x_ref[rows, :], g_ref[...], mod_ref[...]).astype(BF16)
        acc_ref[...] = jnp.zeros_like(acc_ref)

    for rows in _sub_tiles():
        hx = hx_ref[rows, :]
        cu = _dot(hx, wc_ref[...]) * _dot(hx, wu_ref[...])
        y = _dot(hx, wb_ref[...]) * _conv3(cu, cw_ref[...], period)
        y = y * _silu(_dot(hx, wz_ref[...]))
        acc_ref[rows, :] += _dot(y.astype(BF16), wo_ref[...])

    @pl.when(k == pl.num_programs(1) - 1)
    def _():
        for rows in _sub_tiles():
            o_ref[rows, :] = x_ref[rows, :] + mod_ref[:, 2 * D_MODEL:] * acc_ref[rows, :]


def _conv_layer(xs, mod, norm_g, w_in, conv_w, w_out, *, period, tiles_per_batch):
    rows = xs.shape[0]
    nk = D_INNER // COL_BLOCK
    row_spec = pl.BlockSpec((ROW_TILE, D_MODEL), lambda i, k: (i, 0))
    w_in_spec = lambda g: pl.BlockSpec((D_MODEL, COL_BLOCK), lambda i, k: (0, g * nk + k))
    return pl.pallas_call(
        functools.partial(_conv_layer_kernel, period=period),
        grid=(rows // ROW_TILE, nk),
        in_specs=[
            row_spec,
            pl.BlockSpec((None, 1, 3 * D_MODEL), _mod_index(tiles_per_batch)),
            pl.BlockSpec((1, D_MODEL), lambda i, k: (0, 0)),
            w_in_spec(0), w_in_spec(1), w_in_spec(2), w_in_spec(3),
            pl.BlockSpec((3, COL_BLOCK), lambda i, k: (0, k)),
            pl.BlockSpec((COL_BLOCK, D_MODEL), lambda i, k: (k, 0)),
        ],
        out_specs=row_spec,
        out_shape=jax.ShapeDtypeStruct((rows, D_MODEL), F32),
        scratch_shapes=[pltpu.VMEM((ROW_TILE, D_MODEL), BF16), pltpu.VMEM((ROW_TILE, D_MODEL), F32)],
        compiler_params=_params("parallel", "arbitrary"),
        name="conv_layer",
    )(xs, mod, norm_g, w_in, w_in, w_in, w_in, conv_w, w_out)


def _mlstm_proj_kernel(x_ref, mod_ref, g_ref, wu_ref, wz_ref, wo_ref, cw_ref, wq_ref, wk_ref, wv_ref,
                       wg_ref, bg_ref, ng_ref, sk_ref, qt_ref, k_ref, vt_ref, gates_ref, *rest, period, gated):
    hx_ref = rest[-1]
    h = pl.program_id(1)

    @pl.when(h == 0)
    def _():
        for rows in _sub_tiles():
            hx_ref[rows, :] = _modulated_norm(x_ref[rows, :], g_ref[...], mod_ref[...]).astype(BF16)
        gates_ref[...] = jnp.broadcast_to(bg_ref[...], gates_ref.shape)

    for rows in _sub_tiles():
        hx = hx_ref[rows, :]
        u = _dot(hx, wu_ref[...])
        uc = _silu(_conv3(u, cw_ref[...], period))
        ucb = uc.astype(BF16)
        q = _dot(ucb, wq_ref[...])
        v = _dot(u.astype(BF16), wv_ref[...])
        qb = q.astype(BF16)
        kb = (_dot(ucb, wk_ref[...]) * (HEAD_DIM ** -0.5)).astype(BF16)
        vb = v.astype(BF16)
        k_ref[rows, :] = kb
        for piece in range(SUB_ROWS // SCAN_CHUNK):
            chunk = rows.start // SCAN_CHUNK + piece
            part = slice(piece * SCAN_CHUNK, (piece + 1) * SCAN_CHUNK)
            qt_ref[chunk] = q[part, :].T.astype(BF16)
            vt_ref[chunk] = v[part, :].T.astype(BF16)
        gates_ref[rows, :] += _dot(qb, wg_ref[0]) + _dot(kb, wg_ref[1]) + _dot(vb, wg_ref[2])
        if gated:
            a_ref, b_ref = rest[:2]
            sz = _silu(_dot(hx, wz_ref[...]))
            a_ref[rows, :] = (_sigmoid(_dot(hx, wo_ref[...])) * ng_ref[...] * sz).astype(BF16)
            b_ref[rows, :] = (sk_ref[...] * uc * sz).astype(BF16)


def _mlstm_proj(xs, mod, norm_g, w_in, conv_w, wq, wk, wv, wg, bg, m_norm_g, m_skip, *, period,
                tiles_per_batch, gated):
    rows = xs.shape[0]
    w_in_spec = lambda g: pl.BlockSpec((D_MODEL, HEAD_DIM), lambda i, h: (0, g * HEADS + h))
    head_w = pl.BlockSpec((None, HEAD_DIM, HEAD_DIM), lambda i, h: (h, 0, 0))
    head_out = pl.BlockSpec((ROW_TILE, HEAD_DIM), lambda i, h: (i, h))
    head_vec = pl.BlockSpec((1, HEAD_DIM), lambda i, h: (0, h))
    inner = jax.ShapeDtypeStruct((rows, D_INNER), BF16)
    head_t = pl.BlockSpec((ROW_TILE // SCAN_CHUNK, HEAD_DIM, SCAN_CHUNK), lambda i, h: (i, h, 0))
    inner_t = jax.ShapeDtypeStruct((rows // SCAN_CHUNK, D_INNER, SCAN_CHUNK), BF16)
    n_inner = 5 if gated else 3
    outs = pl.pallas_call(
        functools.partial(_mlstm_proj_kernel, period=period, gated=gated),
        grid=(rows // ROW_TILE, HEADS),
        in_specs=[
            pl.BlockSpec((ROW_TILE, D_MODEL), lambda i, h: (i, 0)),
            pl.BlockSpec((None, 1, 3 * D_MODEL), _mod_index(tiles_per_batch)),
            pl.BlockSpec((1, D_MODEL), lambda i, h: (0, 0)),
            w_in_spec(0), w_in_spec(1), w_in_spec(2),
            pl.BlockSpec((3, HEAD_DIM), lambda i, h: (0, h)),
            head_w, head_w, head_w,
            pl.BlockSpec((3, None, HEAD_DIM, 2 * GATE_LANES), lambda i, h: (0, h, 0, 0)),
            pl.BlockSpec((1, 2 * GATE_LANES), lambda i, h: (0, 0)),
            head_vec, head_vec,
        ],
        out_specs=[head_t, head_out, head_t, pl.BlockSpec((ROW_TILE, 2 * GATE_LANES), lambda i, h: (i, 0))]
        + [head_out] * (n_inner - 3),
        out_shape=[inner_t, inner, inner_t, jax.ShapeDtypeStruct((rows, 2 * GATE_LANES), F32)]
        + [inner] * (n_inner - 3),
        scratch_shapes=[pltpu.VMEM((ROW_TILE, D_MODEL), BF16)],
        compiler_params=_params("parallel", "arbitrary"),
        name="mlstm_proj",
    )(xs, mod, norm_g, w_in, w_in, w_in, conv_w, wq, wk, wv, wg, bg, m_norm_g, m_skip)
    return outs


def _log_sigmoid(x):
    return jnp.minimum(x, 0.0) - jnp.log1p(jnp.exp(-jnp.abs(x)))


def _split3(x):
    hi = x.astype(BF16)
    r = x - hi.astype(F32)
    mid = r.astype(BF16)
    lo = (r - mid.astype(F32)).astype(BF16)
    return hi, mid, lo


def _running_max_lanes(x, reverse):
    n = x.shape[-1]
    lane = lax.broadcasted_iota(jnp.int32, x.shape, 1)
    shift = 1
    while shift < n:
        if reverse:
            moved = jnp.where(lane < n - shift, pltpu.roll(x, n - shift, axis=1), -jnp.inf)
        else:
            moved = jnp.where(lane >= shift, pltpu.roll(x, shift, axis=1), -jnp.inf)
        x = jnp.maximum(x, moved)
        shift *= 2
    return x


ROW_CR, ROW_CM, ROW_BCUM, ROW_BLAST = 0, 1, 2, 3
ROW_KINDS = 4


def _gate_prep_kernel(gcol_ref, grow_ref, crc_ref, row_ref, *, nchunks):
    L = SCAN_CHUNK
    G = 2 * HEADS
    r = lax.broadcasted_iota(jnp.int32, (L, L), 0)
    c = lax.broadcasted_iota(jnp.int32, (L, L), 1)
    lower = (c <= r).astype(BF16)
    upper = (c >= r).astype(BF16)
    fwd_lane = lax.broadcasted_iota(jnp.int32, (1, GATE_LANES), 1) < HEADS
    fwd_row = lax.broadcasted_iota(jnp.int32, (G, 1), 0) < HEADS
    for ci in range(nchunks):
        rows = pl.ds(ci * L, L)
        parts = _split3(_log_sigmoid(gcol_ref[rows, GATE_LANES:]))
        pre = sum(_dot(lower, p) for p in parts)
        suf = sum(_dot(upper, p) for p in parts)
        crc_ref[rows, :] = gcol_ref[rows, :GATE_LANES] - jnp.where(fwd_lane, pre, suf)
        parts_r = _split3(_log_sigmoid(grow_ref[:, rows]))
        pre_r = sum(_dot(p, upper) for p in parts_r)[G:, :]
        suf_r = sum(_dot(p, lower) for p in parts_r)[G:, :]
        bcum_r = jnp.where(fwd_row, pre_r, suf_r)
        cr_r = grow_ref[0:G, rows] - bcum_r
        b_last_r = jnp.where(fwd_row, pre_r[:, L - 1:L], suf_r[:, 0:1])
        row_ref[ci, ROW_CR * G:(ROW_CR + 1) * G, :] = cr_r
        row_ref[ci, ROW_CM * G:(ROW_CM + 1) * G, :] = jnp.where(
            fwd_row, _running_max_lanes(cr_r, False), _running_max_lanes(cr_r, True))
        row_ref[ci, ROW_BCUM * G:(ROW_BCUM + 1) * G, :] = bcum_r
        row_ref[ci, ROW_BLAST * G:(ROW_BLAST + 1) * G, :] = jnp.broadcast_to(b_last_r, (G, L))


def _gate_prep(gates_col, gates_row, tokens):
    nchunks = tokens // SCAN_CHUNK
    row_rows = ROW_KINDS * 2 * HEADS
    return pl.pallas_call(
        functools.partial(_gate_prep_kernel, nchunks=nchunks),
        grid=(BATCH,),
        in_specs=[
            pl.BlockSpec((None, tokens, 2 * GATE_LANES), lambda b: (b, 0, 0)),
            pl.BlockSpec((None, 4 * HEADS, tokens), lambda b: (b, 0, 0)),
        ],
        out_specs=[
            pl.BlockSpec((None, tokens, GATE_LANES), lambda b: (b, 0, 0)),
            pl.BlockSpec((None, nchunks, row_rows, SCAN_CHUNK), lambda b: (b, 0, 0, 0)),
        ],
        out_shape=[
            jax.ShapeDtypeStruct((BATCH, tokens, GATE_LANES), F32),
            jax.ShapeDtypeStruct((BATCH, nchunks, row_rows, SCAN_CHUNK), F32),
        ],
        compiler_params=_params("parallel"),
        name="gate_prep",
    )(gates_col, gates_row)


class _Seq:
    def __init__(self, k, qt, vt, crc, row, hs, hn, nchunks):
        self.k, self.qt, self.vt, self.crc, self.row = k, qt, vt, crc, row
        self.hs, self.hn, self.nchunks = hs, hn, nchunks


def _chunk_rows(ci):
    start = ci * SCAN_CHUNK
    return pl.ds(start if isinstance(ci, int) else pl.multiple_of(start, SCAN_CHUNK), SCAN_CHUNK)


def _dot_row_halves(a, b, a2=None, b2=None):
    rows = a.shape[0]
    cut = (rows // 32) * 16
    parts = []
    for part in (slice(0, cut), slice(cut, rows)):
        acc = _dot(a[part], b)
        if a2 is not None:
            acc = acc + _dot(a2[part], b2)
        parts.append(acc)
    return jnp.concatenate(parts, axis=0)


def _direction_step(seq, ci, direction, head, ct_ref, n, m):
    L = SCAN_CHUNK
    rows = _chunk_rows(ci)
    gate = direction * HEADS + head
    row = lambda kind: seq.row[ci, pl.ds(kind * 2 * HEADS + gate, 1), :]
    k = seq.k[rows, :]
    qt = seq.qt[ci]
    vt = seq.vt[ci]
    cr_row, cm_row = row(ROW_CR), row(ROW_CM)
    top_j = jnp.maximum(m, cm_row)
    want_h = seq.hn is not None
    ct = ct_ref[...]

    if want_h:
        n_rows = jnp.broadcast_to(n.astype(BF16), (16, HEAD_DIM))
        kq = _dot_row_halves(jnp.concatenate([k, n_rows], axis=0), qt)

    top = jnp.maximum(m, jnp.max(cr_row, axis=-1, keepdims=True))
    w_decay = jnp.exp(m - top)
    ws = jnp.exp(cr_row - top).astype(BF16)
    upd = _dot(jnp.concatenate([vt * ws, jnp.broadcast_to(ws, (16, L))], axis=0), k)

    h = None
    if want_h:
        lane_sel = lax.broadcasted_iota(jnp.int32, (1, GATE_LANES), 1) == gate
        cr_col = jnp.sum(jnp.where(lane_sel, seq.crc[rows, :], 0.0), axis=-1, keepdims=True)
        s = lax.broadcasted_iota(jnp.int32, (L, L), 0)
        j = lax.broadcasted_iota(jnp.int32, (L, L), 1)
        seen = (s <= j) if direction == 0 else (s >= j)
        decay = jnp.exp(jnp.where(seen, cr_col - cm_row, -jnp.inf))
        sc = kq[:L] * decay
        w_intra = jnp.exp(cm_row - top_j)
        w_inter = jnp.exp(m - top_j)
        den = w_inter * kq[L:L + 1] + w_intra * jnp.sum(sc, axis=0, keepdims=True)
        inv = 1.0 / jnp.maximum(jnp.abs(den), jnp.exp(-(row(ROW_BCUM) + top_j)))
        qs = qt * (w_inter * inv).astype(BF16)
        ss = (sc * (w_intra * inv)).astype(BF16)
        h = _dot_row_halves(ct.astype(BF16), qs, vt, ss)

    ct_ref[...] = w_decay * ct + upd[:HEAD_DIM]
    n_new = w_decay * n + upd[HEAD_DIM:HEAD_DIM + 1]
    m_new = row(ROW_BLAST)[:, 0:1] + top
    return h, n_new, m_new


def _head_norm_t(hsum):
    hc = hsum - jnp.mean(hsum, axis=0, keepdims=True)
    var = jnp.mean(hc * hc, axis=0, keepdims=True)
    return (hc * lax.rsqrt(var + EPS)).T.astype(BF16)


def _scan_stream(seq, direction, head, ct_ref, state):
    nch = seq.nchunks

    def body(i, st):
        ci = i if direction == 0 else nch - 1 - i
        h, n, m = _direction_step(seq, ci, direction, head, ct_ref, *st)
        if seq.hn is not None:
            if direction == 0:
                seq.hs[ci] = h
            else:
                seq.hn[_chunk_rows(ci), :] = _head_norm_t(seq.hs[ci] + h)
        return n, m

    for i in range(nch):
        state = body(i, state)
    return state


def _scan_kernel(*refs, ctx_out):
    kx, qtx, vtx, crcx, rowx, kc, qtc, vtc, crcc, rowc = refs[:10]
    refs = list(refs[10:])
    hnx = refs.pop(0)
    hnc = refs.pop(0) if ctx_out else None
    ct_ref, hsx = refs[:2]
    hsc = refs[2] if ctx_out else None
    head = pl.program_id(1)
    ctx = _Seq(kc, qtc, vtc, crcc, rowc, hsc, hnc, CTX_LEN // SCAN_CHUNK)
    lat = _Seq(kx, qtx, vtx, crcx, rowx, hsx, hnx, SEQ // SCAN_CHUNK)
    for direction in (0, 1):
        ct_ref[...] = jnp.zeros_like(ct_ref)
        state = (jnp.zeros((1, HEAD_DIM), F32), jnp.zeros((1, 1), F32))
        state = _scan_stream(ctx, direction, head, ct_ref, state)
        _scan_stream(lat, direction, head, ct_ref, state)


def _scan(lat, ctx, *, ctx_out):
    def specs(tokens):
        nch = tokens // SCAN_CHUNK
        qv_t = pl.BlockSpec((None, nch, HEAD_DIM, SCAN_CHUNK), lambda b, h: (b, 0, h, 0))
        return [pl.BlockSpec((None, tokens, HEAD_DIM), lambda b, h: (b, 0, h)), qv_t, qv_t,
                pl.BlockSpec((None, tokens, GATE_LANES), lambda b, h: (b, 0, 0)),
                pl.BlockSpec((None, nch, ROW_KINDS * 2 * HEADS, SCAN_CHUNK), lambda b, h: (b, 0, 0, 0))]

    hn_spec = lambda tokens: pl.BlockSpec((None, tokens, HEAD_DIM), lambda b, h: (b, 0, h))
    hn_shape = lambda tokens: jax.ShapeDtypeStruct((BATCH, tokens, D_INNER), BF16)
    hs_shape = lambda tokens: pltpu.VMEM((tokens // SCAN_CHUNK, HEAD_DIM, SCAN_CHUNK), F32)
    out_specs = [hn_spec(SEQ)] + ([hn_spec(CTX_LEN)] if ctx_out else [])
    out_shape = [hn_shape(SEQ)] + ([hn_shape(CTX_LEN)] if ctx_out else [])
    scratch = [pltpu.VMEM((HEAD_DIM, HEAD_DIM), F32), hs_shape(SEQ)] + ([hs_shape(CTX_LEN)] if ctx_out else [])
    outs = pl.pallas_call(
        functools.partial(_scan_kernel, ctx_out=ctx_out),
        grid=(BATCH, HEADS),
        in_specs=specs(SEQ) + specs(CTX_LEN),
        out_specs=out_specs,
        out_shape=out_shape,
        scratch_shapes=scratch,
        compiler_params=_params("parallel", "arbitrary"),
        name="mlstm_scan",
    )(*lat, *ctx)
    return outs if ctx_out else (outs[0], None)


def _mlstm_out_kernel(x_ref, mod_ref, hn_ref, a_ref, b_ref, wo_ref, fg_ref, out_ref, *, final):
    for r in range(OUT_ROWS // OUT_SUB_ROWS):
        rows = pl.ds(r * OUT_SUB_ROWS, OUT_SUB_ROWS)
        y = hn_ref[rows, :].astype(F32) * a_ref[rows, :].astype(F32) + b_ref[rows, :].astype(F32)
        xn = x_ref[rows, :] + mod_ref[:, 2 * D_MODEL:] * _dot(y.astype(BF16), wo_ref[...])
        if final:
            xn = xn * lax.rsqrt(jnp.mean(xn * xn, axis=-1, keepdims=True) + EPS) * fg_ref[...]
        out_ref[rows, :] = xn


def _mlstm_out(xs, mod, hn, a, b, w_out, final_g, *, tiles_per_batch, final):
    rows = xs.shape[0]
    row_spec = pl.BlockSpec((OUT_ROWS, D_MODEL), lambda i: (i, 0))
    inner = pl.BlockSpec((OUT_ROWS, D_INNER), lambda i: (i, 0))
    return pl.pallas_call(
        functools.partial(_mlstm_out_kernel, final=final),
        grid=(rows // OUT_ROWS,),
        in_specs=[
            row_spec,
            pl.BlockSpec((None, 1, 3 * D_MODEL), _mod_index(tiles_per_batch)),
            inner, inner, inner,
            pl.BlockSpec((D_INNER, D_MODEL), lambda i: (0, 0)),
            pl.BlockSpec((1, D_MODEL), lambda i: (0, 0)),
        ],
        out_specs=row_spec,
        out_shape=jax.ShapeDtypeStruct((rows, D_MODEL), F32),
        compiler_params=_params("parallel"),
        name="mlstm_out",
    )(xs, mod, hn, a, b, w_out, final_g)


def _gate_weights(w_gate, b_gate):
    def place(a):
        a = a.reshape(a.shape[:-1] + (4, HEADS))
        ig = a[..., 0::2, :].reshape(a.shape[:-2] + (2 * HEADS,))
        fg = a[..., 1::2, :].reshape(a.shape[:-2] + (2 * HEADS,))
        pad = [(0, 0)] * (ig.ndim - 1) + [(0, GATE_LANES - 2 * HEADS)]
        return jnp.concatenate([jnp.pad(ig, pad), jnp.pad(fg, pad)], axis=-1)

    wg = place(w_gate).reshape(3, HEADS, HEAD_DIM, 2 * GATE_LANES).astype(BF16)
    bg = place(b_gate).reshape(1, 2 * GATE_LANES)
    return wg, bg


def _gate_layouts(gates, tokens):
    col = gates.reshape(BATCH, tokens, 2 * GATE_LANES)
    used = jnp.concatenate([col[..., :2 * HEADS], col[..., GATE_LANES:GATE_LANES + 2 * HEADS]], axis=-1)
    return col, jnp.swapaxes(used, 1, 2)


def kernel(x, c, ctx, c_ctx, norm_g, mod_w, mod_b, conv_w_in, conv_w, conv_w_out, m_w_in, m_conv_w, m_wq,
           m_wk, m_wv, m_w_gate, m_b_gate, m_norm_g, m_skip, m_w_out, final_g):
    assert x.shape == (BATCH, SEQ, D_MODEL) and ctx.shape == (BATCH, CTX_LEN, D_MODEL)
    c_all = jnp.zeros((MOD_ROWS, D_MODEL), F32).at[:BATCH].set(c).at[CTX_MOD_ROW].set(c_ctx)
    mods = _modulation(c_all, mod_w, mod_b).reshape(DEPTH, MOD_ROWS, 1, 3 * D_MODEL)
    xs = x.reshape(BATCH * SEQ, D_MODEL)
    cs = ctx.reshape(BATCH * CTX_LEN, D_MODEL)
    lat_tiles = SEQ // ROW_TILE
    final_g2 = final_g.reshape(1, D_MODEL)
    for i in range(DEPTH):
        last = i == DEPTH - 1
        j = i // 2
        g = norm_g[i].reshape(1, D_MODEL)
        if i % 2 == 0:
            w_in, w_out = conv_w_in[j].astype(BF16), conv_w_out[j].astype(BF16)
            xs = _conv_layer(xs, mods[i], g, w_in, conv_w[j], w_out, period=GRID_W, tiles_per_batch=lat_tiles)
            if not last:
                cs = _conv_layer(cs, mods[i], g, w_in, conv_w[j], w_out, period=CTX_LEN, tiles_per_batch=None)
        else:
            w_in, w_out = m_w_in[j].astype(BF16), m_w_out[j].astype(BF16)
            wq, wk, wv = m_wq[j].astype(BF16), m_wk[j].astype(BF16), m_wv[j].astype(BF16)
            wg, bg = _gate_weights(m_w_gate[j], m_b_gate[j])
            proj = functools.partial(_mlstm_proj, norm_g=g, w_in=w_in, conv_w=m_conv_w[j], wq=wq, wk=wk, wv=wv,
                                     wg=wg, bg=bg, m_norm_g=m_norm_g[j].reshape(1, D_INNER),
                                     m_skip=m_skip[j].reshape(1, D_INNER))
            px = proj(xs, mods[i], period=GRID_W, tiles_per_batch=lat_tiles, gated=True)
            pc = proj(cs, mods[i], period=CTX_LEN, tiles_per_batch=None, gated=not last)

            def stream(p, tokens):
                qt, k, vt, gates = p[:4]
                t_shape = (BATCH, tokens // SCAN_CHUNK, D_INNER, SCAN_CHUNK)
                return [k.reshape(BATCH, tokens, D_INNER), qt.reshape(t_shape), vt.reshape(t_shape)] + list(
                    _gate_prep(*_gate_layouts(gates, tokens), tokens))

            hnx, hnc = _scan(stream(px, SEQ), stream(pc, CTX_LEN), ctx_out=not last)
            out = functools.partial(_mlstm_out, w_out=w_out, final_g=final_g2)
            if not last:
                cs = out(cs, mods[i], hnc.reshape(-1, D_INNER), *pc[4:], tiles_per_batch=None, final=False)
            xs = out(xs, mods[i], hnx.reshape(-1, D_INNER), *px[4:], tiles_per_batch=SEQ // OUT_ROWS, final=last)
    return xs.reshape(BATCH, SEQ, D_MODEL)
```

```python
import functools

import jax
import jax.numpy as jnp
from jax import lax
from jax.experimental import pallas as pl
from jax.experimental.pallas import tpu as pltpu

D_MODEL = 1024
BATCH = 16
SEQ = 2048
DEPTH = 4
GRID_W = 64
CTX_LEN = 256
D_INNER = 2 * D_MODEL
HEADS = 4
HEAD_DIM = D_INNER // HEADS
EPS = 1e-6

SCAN_CHUNK = 256
COL_BLOCK = 512
ROW_TILE = 1024
SUB_ROWS = 512
OUT_ROWS = 512
OUT_SUB_ROWS = 256
MOD_ROWS = 24
CTX_MOD_ROW = BATCH
GATE_LANES = 128
N_GATES = 4 * HEADS
VMEM_LIMIT_BYTES = 56 * 1024 * 1024

F32 = jnp.float32
BF16 = jnp.bfloat16


def _dot(a, b):
    return jnp.dot(a, b, preferred_element_type=F32)


def _sigmoid(x):
    return 0.5 * jnp.tanh(0.5 * x) + 0.5


def _silu(x):
    return x * _sigmoid(x)


def _params(*sem):
    return pltpu.CompilerParams(dimension_semantics=sem, vmem_limit_bytes=VMEM_LIMIT_BYTES)


def _mod_kernel(c_ref, w_ref, b_ref, o_ref):
    sc = _silu(c_ref[...])
    o_ref[...] = jnp.dot(sc, w_ref[...], preferred_element_type=F32,
                         precision=lax.Precision.HIGHEST) + b_ref[...]


def _modulation(c_all, mod_w, mod_b):
    nblk = 3 * D_MODEL // D_MODEL
    return pl.pallas_call(
        _mod_kernel,
        grid=(DEPTH, nblk),
        in_specs=[
            pl.BlockSpec((MOD_ROWS, D_MODEL), lambda l, j: (0, 0)),
            pl.BlockSpec((None, D_MODEL, D_MODEL), lambda l, j: (l, 0, j)),
            pl.BlockSpec((None, 1, D_MODEL), lambda l, j: (l, 0, j)),
        ],
        out_specs=pl.BlockSpec((None, MOD_ROWS, D_MODEL), lambda l, j: (l, 0, j)),
        out_shape=jax.ShapeDtypeStruct((DEPTH, MOD_ROWS, 3 * D_MODEL), F32),
        compiler_params=_params("arbitrary", "arbitrary"),
        name="modulation",
    )(c_all, mod_w, mod_b.reshape(DEPTH, 1, 3 * D_MODEL))


def _modulated_norm(x, g, mod):
    y = x * lax.rsqrt(jnp.mean(x * x, axis=-1, keepdims=True) + EPS) * g
    return y * (1.0 + mod[:, D_MODEL:2 * D_MODEL]) + mod[:, :D_MODEL]


def _conv3(u, w, period):
    rows = u.shape[0]
    t = lax.broadcasted_iota(jnp.int32, (rows, 1), 0) & (period - 1)
    prev = jnp.where(t != 0, pltpu.roll(u, 1, axis=0), 0.0)
    nxt = jnp.where(t != period - 1, pltpu.roll(u, rows - 1, axis=0), 0.0)
    return prev * w[0:1, :] + u * w[1:2, :] + nxt * w[2:3, :]


def _sub_tiles():
    return [pl.ds(r * SUB_ROWS, SUB_ROWS) for r in range(ROW_TILE // SUB_ROWS)]


def _mod_index(tiles_per_batch):
    if tiles_per_batch is None:
        return lambda i, *_: (CTX_MOD_ROW, 0, 0)
    return lambda i, *_: (i // tiles_per_batch, 0, 0)


def _conv_layer_kernel(x_ref, mod_ref, g_ref, wb_ref, wc_ref, wu_ref, wz_ref, cw_ref, wo_ref,
                       o_ref, hx_ref, acc_ref, *, period):
    k = pl.program_id(1)

    @pl.when(k == 0)
    def _():
        for rows in _sub_tiles():
            hx_ref[rows, :] = _modulated_norm(x_ref[rows, :], g_ref[...], mod_ref[...]).astype(BF16)
        acc_ref[...] = jnp.zeros_like(acc_ref)

    for rows in _sub_tiles():
        hx = hx_ref[rows, :]
        cu = _dot(hx, wc_ref[...]) * _dot(hx, wu_ref[...])
        y = _dot(hx, wb_ref[...]) * _conv3(cu, cw_ref[...], period)
        y = y * _silu(_dot(hx, wz_ref[...]))
        acc_ref[rows, :] += _dot(y.astype(BF16), wo_ref[...])

    @pl.when(k == pl.num_programs(1) - 1)
    def _():
        for rows in _sub_tiles():
            o_ref[rows, :] = x_ref[rows, :] + mod_ref[:, 2 * D_MODEL:] * acc_ref[rows, :]


def _conv_layer(xs, mod, norm_g, w_in, conv_w, w_out, *, period, tiles_per_batch):
    rows = xs.shape[0]
    nk = D_INNER // COL_BLOCK
    row_spec = pl.BlockSpec((ROW_TILE, D_MODEL), lambda i, k: (i, 0))
    w_in_spec = lambda g: pl.BlockSpec((D_MODEL, COL_BLOCK), lambda i, k: (0, g * nk + k))
    return pl.pallas_call(
        functools.partial(_conv_layer_kernel, period=period),
        grid=(rows // ROW_TILE, nk),
        in_specs=[
            row_spec,
            pl.BlockSpec((None, 1, 3 * D_MODEL), _mod_index(tiles_per_batch)),
            pl.BlockSpec((1, D_MODEL), lambda i, k: (0, 0)),
            w_in_spec(0), w_in_spec(1), w_in_spec(2), w_in_spec(3),
            pl.BlockSpec((3, COL_BLOCK), lambda i, k: (0, k)),
            pl.BlockSpec((COL_BLOCK, D_MODEL), lambda i, k: (k, 0)),
        ],
        out_specs=row_spec,
        out_shape=jax.ShapeDtypeStruct((rows, D_MODEL), F32),
        scratch_shapes=[pltpu.VMEM((ROW_TILE, D_MODEL), BF16), pltpu.VMEM((ROW_TILE, D_MODEL), F32)],
        compiler_params=_params("parallel", "arbitrary"),
        name="conv_layer",
    )(xs, mod, norm_g, w_in, w_in, w_in, w_in, conv_w, w_out)


def _mlstm_proj_kernel(x_ref, mod_ref, g_ref, wu_ref, wz_ref, wo_ref, cw_ref, wq_ref, wk_ref, wv_ref,
                       wg_ref, bg_ref, ng_ref, sk_ref, qt_ref, k_ref, vt_ref, gates_ref, *rest, period, gated):
    hx_ref = rest[-1]
    h = pl.program_id(1)

    @pl.when(h == 0)
    def _():
        for rows in _sub_tiles():
            hx_ref[rows, :] = _modulated_norm(x_ref[rows, :], g_ref[...], mod_ref[...]).astype(BF16)
        gates_ref[...] = jnp.broadcast_to(bg_ref[...], gates_ref.shape)

    for rows in _sub_tiles():
        hx = hx_ref[rows, :]
        u = _dot(hx, wu_ref[...])
        uc = _silu(_conv3(u, cw_ref[...], period))
        ucb = uc.astype(BF16)
        q = _dot(ucb, wq_ref[...])
        k = _dot(ucb, wk_ref[...]) * (HEAD_DIM ** -0.5)
        v = _dot(u.astype(BF16), wv_ref[...])
        k_ref[rows, :] = k.astype(BF16)
        for piece in range(SUB_ROWS // SCAN_CHUNK):
            chunk = rows.start // SCAN_CHUNK + piece
            part = slice(piece * SCAN_CHUNK, (piece + 1) * SCAN_CHUNK)
            qt = q[part, :].T.astype(BF16)
            kt = k[part, :].T.astype(BF16)
            vt = v[part, :].T.astype(BF16)
            qt_ref[chunk] = qt
            vt_ref[chunk] = vt
            cols = pl.ds(chunk * SCAN_CHUNK, SCAN_CHUNK)
            gates_ref[:, cols] += _dot(wg_ref[0], qt) + _dot(wg_ref[1], kt) + _dot(wg_ref[2], vt)
        if gated:
            a_ref, b_ref = rest[:2]
            sz = _silu(_dot(hx, wz_ref[...]))
            a_ref[rows, :] = (_sigmoid(_dot(hx, wo_ref[...])) * ng_ref[...] * sz).astype(BF16)
            b_ref[rows, :] = (sk_ref[...] * uc * sz).astype(BF16)


def _mlstm_proj(xs, mod, norm_g, w_in, conv_w, wq, wk, wv, wg, bg, m_norm_g, m_skip, *, period,
                tiles_per_batch, gated):
    rows = xs.shape[0]
    w_in_spec = lambda g: pl.BlockSpec((D_MODEL, HEAD_DIM), lambda i, h: (0, g * HEADS + h))
    head_w = pl.BlockSpec((None, HEAD_DIM, HEAD_DIM), lambda i, h: (h, 0, 0))
    head_out = pl.BlockSpec((ROW_TILE, HEAD_DIM), lambda i, h: (i, h))
    head_vec = pl.BlockSpec((1, HEAD_DIM), lambda i, h: (0, h))
    inner = jax.ShapeDtypeStruct((rows, D_INNER), BF16)
    head_t = pl.BlockSpec((ROW_TILE // SCAN_CHUNK, HEAD_DIM, SCAN_CHUNK), lambda i, h: (i, h, 0))
    inner_t = jax.ShapeDtypeStruct((rows // SCAN_CHUNK, D_INNER, SCAN_CHUNK), BF16)
    n_inner = 5 if gated else 3
    outs = pl.pallas_call(
        functools.partial(_mlstm_proj_kernel, period=period, gated=gated),
        grid=(rows // ROW_TILE, HEADS),
        in_specs=[
            pl.BlockSpec((ROW_TILE, D_MODEL), lambda i, h: (i, 0)),
            pl.BlockSpec((None, 1, 3 * D_MODEL), _mod_index(tiles_per_batch)),
            pl.BlockSpec((1, D_MODEL), lambda i, h: (0, 0)),
            w_in_spec(0), w_in_spec(1), w_in_spec(2),
            pl.BlockSpec((3, HEAD_DIM), lambda i, h: (0, h)),
            head_w, head_w, head_w,
            pl.BlockSpec((3, None, N_GATES, HEAD_DIM), lambda i, h: (0, h, 0, 0)),
            pl.BlockSpec((N_GATES, 1), lambda i, h: (0, 0)),
            head_vec, head_vec,
        ],
        out_specs=[head_t, head_out, head_t, pl.BlockSpec((N_GATES, ROW_TILE), lambda i, h: (0, i))]
        + [head_out] * (n_inner - 3),
        out_shape=[inner_t, inner, inner_t, jax.ShapeDtypeStruct((N_GATES, rows), F32)]
        + [inner] * (n_inner - 3),
        scratch_shapes=[pltpu.VMEM((ROW_TILE, D_MODEL), BF16)],
        compiler_params=_params("parallel", "arbitrary"),
        name="mlstm_proj",
    )(xs, mod, norm_g, w_in, w_in, w_in, conv_w, wq, wk, wv, wg, bg, m_norm_g, m_skip)
    return outs


def _log_sigmoid(x):
    return jnp.minimum(x, 0.0) - jnp.log1p(jnp.exp(-jnp.abs(x)))


def _split3(x):
    hi = x.astype(BF16)
    r = x - hi.astype(F32)
    mid = r.astype(BF16)
    lo = (r - mid.astype(F32)).astype(BF16)
    return hi, mid, lo


def _running_max_lanes(x, reverse):
    n = x.shape[-1]
    lane = lax.broadcasted_iota(jnp.int32, x.shape, 1)
    shift = 1
    while shift < n:
        if reverse:
            moved = jnp.where(lane < n - shift, pltpu.roll(x, n - shift, axis=1), -jnp.inf)
        else:
            moved = jnp.where(lane >= shift, pltpu.roll(x, shift, axis=1), -jnp.inf)
        x = jnp.maximum(x, moved)
        shift *= 2
    return x


ROW_CR, ROW_CM, ROW_BCUM, ROW_BLAST = 0, 1, 2, 3
ROW_KINDS = 4


def _gate_prep_kernel(g_ref, crc_ref, row_ref, *, nchunks):
    L = SCAN_CHUNK
    G = 2 * HEADS
    r = lax.broadcasted_iota(jnp.int32, (L, L), 0)
    c = lax.broadcasted_iota(jnp.int32, (L, L), 1)
    lower = (c <= r).astype(BF16)
    upper = (c >= r).astype(BF16)
    fwd_row = lax.broadcasted_iota(jnp.int32, (G, 1), 0) < HEADS
    for ci in range(nchunks):
        rows = pl.ds(ci * L, L)
        g = g_ref[:, rows]
        parts = _split3(_log_sigmoid(g))
        pre = sum(_dot(p, upper) for p in parts)[G:, :]
        suf = sum(_dot(p, lower) for p in parts)[G:, :]
        bcum = jnp.where(fwd_row, pre, suf)
        cr = g[0:G, :] - bcum
        b_last = jnp.where(fwd_row, pre[:, L - 1:L], suf[:, 0:1])
        row_ref[ci, ROW_CR * G:(ROW_CR + 1) * G, :] = cr
        row_ref[ci, ROW_CM * G:(ROW_CM + 1) * G, :] = jnp.where(
            fwd_row, _running_max_lanes(cr, False), _running_max_lanes(cr, True))
        row_ref[ci, ROW_BCUM * G:(ROW_BCUM + 1) * G, :] = bcum
        row_ref[ci, ROW_BLAST * G:(ROW_BLAST + 1) * G, :] = jnp.broadcast_to(b_last, (G, L))
        crc_ref[rows, :] = jnp.concatenate([cr, jnp.zeros((GATE_LANES - G, L), F32)], axis=0).T


def _gate_prep(gates, tokens):
    nchunks = tokens // SCAN_CHUNK
    row_rows = ROW_KINDS * 2 * HEADS
    return pl.pallas_call(
        functools.partial(_gate_prep_kernel, nchunks=nchunks),
        grid=(BATCH,),
        in_specs=[pl.BlockSpec((N_GATES, tokens), lambda b: (0, b))],
        out_specs=[
            pl.BlockSpec((None, tokens, GATE_LANES), lambda b: (b, 0, 0)),
            pl.BlockSpec((None, nchunks, row_rows, SCAN_CHUNK), lambda b: (b, 0, 0, 0)),
        ],
        out_shape=[
            jax.ShapeDtypeStruct((BATCH, tokens, GATE_LANES), F32),
            jax.ShapeDtypeStruct((BATCH, nchunks, row_rows, SCAN_CHUNK), F32),
        ],
        compiler_params=_params("parallel"),
        name="gate_prep",
    )(gates)


class _Seq:
    def __init__(self, k, qt, vt, crc, row, hs, hn, nchunks):
        self.k, self.qt, self.vt, self.crc, self.row = k, qt, vt, crc, row
        self.hs, self.hn, self.nchunks = hs, hn, nchunks


def _chunk_rows(ci):
    start = ci * SCAN_CHUNK
    return pl.ds(start if isinstance(ci, int) else pl.multiple_of(start, SCAN_CHUNK), SCAN_CHUNK)


def _dot_row_halves(a, b, a2=None, b2=None):
    rows = a.shape[0]
    cut = (rows // 32) * 16
    parts = []
    for part in (slice(0, cut), slice(cut, rows)):
        acc = _dot(a[part], b)
        if a2 is not None:
            acc = acc + _dot(a2[part], b2)
        parts.append(acc)
    return jnp.concatenate(parts, axis=0)


def _direction_step(seq, ci, direction, head, ct_ref, n, m):
    L = SCAN_CHUNK
    rows = _chunk_rows(ci)
    gate = direction * HEADS + head
    row = lambda kind: seq.row[ci, pl.ds(kind * 2 * HEADS + gate, 1), :]
    k = seq.k[rows, :]
    qt = seq.qt[ci]
    vt = seq.vt[ci]
    cr_row, cm_row = row(ROW_CR), row(ROW_CM)
    top_j = jnp.maximum(m, cm_row)
    want_h = seq.hn is not None
    ct = ct_ref[...]

    if want_h:
        n_rows = jnp.broadcast_to(n.astype(BF16), (16, HEAD_DIM))
        kq = _dot_row_halves(jnp.concatenate([k, n_rows], axis=0), qt)

    top = jnp.maximum(m, jnp.max(cr_row, axis=-1, keepdims=True))
    w_decay = jnp.exp(m - top)
    ws = jnp.exp(cr_row - top).astype(BF16)
    upd = _dot(jnp.concatenate([vt * ws, jnp.broadcast_to(ws, (16, L))], axis=0), k)

    h = None
    if want_h:
        lane_sel = lax.broadcasted_iota(jnp.int32, (1, GATE_LANES), 1) == gate
        cr_col = jnp.sum(jnp.where(lane_sel, seq.crc[rows, :], 0.0), axis=-1, keepdims=True)
        s = lax.broadcasted_iota(jnp.int32, (L, L), 0)
        j = lax.broadcasted_iota(jnp.int32, (L, L), 1)
        seen = (s <= j) if direction == 0 else (s >= j)
        decay = jnp.exp(jnp.where(seen, cr_col - cm_row, -jnp.inf))
        sc = kq[:L] * decay
        w_intra = jnp.exp(cm_row - top_j)
        w_inter = jnp.exp(m - top_j)
        den = w_inter * kq[L:L + 1] + w_intra * jnp.sum(sc, axis=0, keepdims=True)
        inv = 1.0 / jnp.maximum(jnp.abs(den), jnp.exp(-(row(ROW_BCUM) + top_j)))
        qs = qt * (w_inter * inv).astype(BF16)
        ss = (sc * (w_intra * inv)).astype(BF16)
        h = _dot_row_halves(ct.astype(BF16), qs, vt, ss)

    ct_ref[...] = w_decay * ct + upd[:HEAD_DIM]
    n_new = w_decay * n + upd[HEAD_DIM:HEAD_DIM + 1]
    m_new = row(ROW_BLAST)[:, 0:1] + top
    return h, n_new, m_new


def _head_norm_t(hsum):
    hc = hsum - jnp.mean(hsum, axis=0, keepdims=True)
    var = jnp.mean(hc * hc, axis=0, keepdims=True)
    return (hc * lax.rsqrt(var + EPS)).T.astype(BF16)


def _scan_stream(seq, direction, head, ct_ref, state):
    nch = seq.nchunks

    def body(i, st):
        ci = i if direction == 0 else nch - 1 - i
        h, n, m = _direction_step(seq, ci, direction, head, ct_ref, *st)
        if seq.hn is not None:
            if direction == 0:
                seq.hs[ci] = h
            else:
                seq.hn[_chunk_rows(ci), :] = _head_norm_t(seq.hs[ci] + h)
        return n, m

    for i in range(nch):
        state = body(i, state)
    return state


def _scan_kernel(*refs, ctx_out):
    kx, qtx, vtx, crcx, rowx, kc, qtc, vtc, crcc, rowc = refs[:10]
    refs = list(refs[10:])
    hnx = refs.pop(0)
    hnc = refs.pop(0) if ctx_out else None
    ct_ref, hsx = refs[:2]
    hsc = refs[2] if ctx_out else None
    head = pl.program_id(1)
    ctx = _Seq(kc, qtc, vtc, crcc, rowc, hsc, hnc, CTX_LEN // SCAN_CHUNK)
    lat = _Seq(kx, qtx, vtx, crcx, rowx, hsx, hnx, SEQ // SCAN_CHUNK)
    for direction in (0, 1):
        ct_ref[...] = jnp.zeros_like(ct_ref)
        state = (jnp.zeros((1, HEAD_DIM), F32), jnp.zeros((1, 1), F32))
        state = _scan_stream(ctx, direction, head, ct_ref, state)
        _scan_stream(lat, direction, head, ct_ref, state)


def _scan(lat, ctx, *, ctx_out):
    def specs(tokens):
        nch = tokens // SCAN_CHUNK
        qv_t = pl.BlockSpec((None, nch, HEAD_DIM, SCAN_CHUNK), lambda b, h: (b, 0, h, 0))
        return [pl.BlockSpec((None, tokens, HEAD_DIM), lambda b, h: (b, 0, h)), qv_t, qv_t,
                pl.BlockSpec((None, tokens, GATE_LANES), lambda b, h: (b, 0, 0)),
                pl.BlockSpec((None, nch, ROW_KINDS * 2 * HEADS, SCAN_CHUNK), lambda b, h: (b, 0, 0, 0))]

    hn_spec = lambda tokens: pl.BlockSpec((None, tokens, HEAD_DIM), lambda b, h: (b, 0, h))
    hn_shape = lambda tokens: jax.ShapeDtypeStruct((BATCH, tokens, D_INNER), BF16)
    hs_shape = lambda tokens: pltpu.VMEM((tokens // SCAN_CHUNK, HEAD_DIM, SCAN_CHUNK), F32)
    out_specs = [hn_spec(SEQ)] + ([hn_spec(CTX_LEN)] if ctx_out else [])
    out_shape = [hn_shape(SEQ)] + ([hn_shape(CTX_LEN)] if ctx_out else [])
    scratch = [pltpu.VMEM((HEAD_DIM, HEAD_DIM), F32), hs_shape(SEQ)] + ([hs_shape(CTX_LEN)] if ctx_out else [])
    outs = pl.pallas_call(
        functools.partial(_scan_kernel, ctx_out=ctx_out),
        grid=(BATCH, HEADS),
        in_specs=specs(SEQ) + specs(CTX_LEN),
        out_specs=out_specs,
        out_shape=out_shape,
        scratch_shapes=scratch,
        compiler_params=_params("parallel", "arbitrary"),
        name="mlstm_scan",
    )(*lat, *ctx)
    return outs if ctx_out else (outs[0], None)


def _mlstm_out_kernel(x_ref, mod_ref, hn_ref, a_ref, b_ref, wo_ref, fg_ref, out_ref, *, final):
    for r in range(OUT_ROWS // OUT_SUB_ROWS):
        rows = pl.ds(r * OUT_SUB_ROWS, OUT_SUB_ROWS)
        y = hn_ref[rows, :].astype(F32) * a_ref[rows, :].astype(F32) + b_ref[rows, :].astype(F32)
        xn = x_ref[rows, :] + mod_ref[:, 2 * D_MODEL:] * _dot(y.astype(BF16), wo_ref[...])
        if final:
            xn = xn * lax.rsqrt(jnp.mean(xn * xn, axis=-1, keepdims=True) + EPS) * fg_ref[...]
        out_ref[rows, :] = xn


def _mlstm_out(xs, mod, hn, a, b, w_out, final_g, *, tiles_per_batch, final):
    rows = xs.shape[0]
    row_spec = pl.BlockSpec((OUT_ROWS, D_MODEL), lambda i: (i, 0))
    inner = pl.BlockSpec((OUT_ROWS, D_INNER), lambda i: (i, 0))
    return pl.pallas_call(
        functools.partial(_mlstm_out_kernel, final=final),
        grid=(rows // OUT_ROWS,),
        in_specs=[
            row_spec,
            pl.BlockSpec((None, 1, 3 * D_MODEL), _mod_index(tiles_per_batch)),
            inner, inner, inner,
            pl.BlockSpec((D_INNER, D_MODEL), lambda i: (0, 0)),
            pl.BlockSpec((1, D_MODEL), lambda i: (0, 0)),
        ],
        out_specs=row_spec,
        out_shape=jax.ShapeDtypeStruct((rows, D_MODEL), F32),
        compiler_params=_params("parallel"),
        name="mlstm_out",
    )(xs, mod, hn, a, b, w_out, final_g)


def _gate_weights(w_gate, b_gate):
    def order(a):
        a = a.reshape(a.shape[:-1] + (4, HEADS))
        return jnp.concatenate([a[..., 0::2, :], a[..., 1::2, :]], axis=-2).reshape(a.shape[:-2] + (N_GATES,))

    wg = jnp.swapaxes(order(w_gate).reshape(3, HEADS, HEAD_DIM, N_GATES), -1, -2).astype(BF16)
    return wg, order(b_gate).reshape(N_GATES, 1)


def kernel(x, c, ctx, c_ctx, norm_g, mod_w, mod_b, conv_w_in, conv_w, conv_w_out, m_w_in, m_conv_w, m_wq,
           m_wk, m_wv, m_w_gate, m_b_gate, m_norm_g, m_skip, m_w_out, final_g):
    assert x.shape == (BATCH, SEQ, D_MODEL) and ctx.shape == (BATCH, CTX_LEN, D_MODEL)
    c_all = jnp.zeros((MOD_ROWS, D_MODEL), F32).at[:BATCH].set(c).at[CTX_MOD_ROW].set(c_ctx)
    mods = _modulation(c_all, mod_w, mod_b).reshape(DEPTH, MOD_ROWS, 1, 3 * D_MODEL)
    xs = x.reshape(BATCH * SEQ, D_MODEL)
    cs = ctx.reshape(BATCH * CTX_LEN, D_MODEL)
    lat_tiles = SEQ // ROW_TILE
    final_g2 = final_g.reshape(1, D_MODEL)
    for i in range(DEPTH):
        last = i == DEPTH - 1
        j = i // 2
        g = norm_g[i].reshape(1, D_MODEL)
        if i % 2 == 0:
            w_in, w_out = conv_w_in[j].astype(BF16), conv_w_out[j].astype(BF16)
            xs = _conv_layer(xs, mods[i], g, w_in, conv_w[j], w_out, period=GRID_W, tiles_per_batch=lat_tiles)
            if not last:
                cs = _conv_layer(cs, mods[i], g, w_in, conv_w[j], w_out, period=CTX_LEN, tiles_per_batch=None)
        else:
            w_in, w_out = m_w_in[j].astype(BF16), m_w_out[j].astype(BF16)
            wq, wk, wv = m_wq[j].astype(BF16), m_wk[j].astype(BF16), m_wv[j].astype(BF16)
            wg, bg = _gate_weights(m_w_gate[j], m_b_gate[j])
            proj = functools.partial(_mlstm_proj, norm_g=g, w_in=w_in, conv_w=m_conv_w[j], wq=wq, wk=wk, wv=wv,
                                     wg=wg, bg=bg, m_norm_g=m_norm_g[j].reshape(1, D_INNER),
                                     m_skip=m_skip[j].reshape(1, D_INNER))
            px = proj(xs, mods[i], period=GRID_W, tiles_per_batch=lat_tiles, gated=True)
            pc = proj(cs, mods[i], period=CTX_LEN, tiles_per_batch=None, gated=not last)

            def stream(p, tokens):
                qt, k, vt, gates = p[:4]
                t_shape = (BATCH, tokens // SCAN_CHUNK, D_INNER, SCAN_CHUNK)
                return [k.reshape(BATCH, tokens, D_INNER), qt.reshape(t_shape), vt.reshape(t_shape)] + list(
                    _gate_prep(gates, tokens))

            hnx, hnc = _scan(stream(px, SEQ), stream(pc, CTX_LEN), ctx_out=not last)
            out = functools.partial(_mlstm_out, w_out=w_out, final_g=final_g2)
            if not last:
                cs = out(cs, mods[i], hnc.reshape(-1, D_INNER), *pc[4:], tiles_per_batch=None, final=False)
            xs = out(xs, mods[i], hnx.reshape(-1, D_INNER), *px[4:], tiles_per_batch=SEQ // OUT_ROWS, final=last)
    return xs.reshape(BATCH, SEQ, D_MODEL)
```

```python
import functools

import jax
import jax.numpy as jnp
from jax import lax
from jax.experimental import pallas as pl
from jax.experimental.pallas import tpu as pltpu

D_MODEL = 1024
BATCH = 16
SEQ = 2048
DEPTH = 4
GRID_W = 64
CTX_LEN = 256
D_INNER = 2 * D_MODEL
HEADS = 4
HEAD_DIM = D_INNER // HEADS
EPS = 1e-6

SCAN_CHUNK = 256
COL_BLOCK = 512
ROW_TILE = 1024
SUB_ROWS = 512
OUT_ROWS = 512
OUT_SUB_ROWS = 256
MOD_ROWS = 24
CTX_MOD_ROW = BATCH
GATE_LANES = 128
N_GATES = 4 * HEADS
VMEM_LIMIT_BYTES = 56 * 1024 * 1024

F32 = jnp.float32
BF16 = jnp.bfloat16


def _dot(a, b):
    return jnp.dot(a, b, preferred_element_type=F32)


def _sigmoid(x):
    return 0.5 * jnp.tanh(0.5 * x) + 0.5


def _silu(x):
    return x * _sigmoid(x)


def _params(*sem):
    return pltpu.CompilerParams(dimension_semantics=sem, vmem_limit_bytes=VMEM_LIMIT_BYTES)


def _mod_kernel(c_ref, w_ref, b_ref, o_ref):
    sc = _silu(c_ref[...])
    o_ref[...] = jnp.dot(sc, w_ref[...], preferred_element_type=F32,
                         precision=lax.Precision.HIGHEST) + b_ref[...]


def _modulation(c_all, mod_w, mod_b):
    nblk = 3 * D_MODEL // D_MODEL
    return pl.pallas_call(
        _mod_kernel,
        grid=(DEPTH, nblk),
        in_specs=[
            pl.BlockSpec((MOD_ROWS, D_MODEL), lambda l, j: (0, 0)),
            pl.BlockSpec((None, D_MODEL, D_MODEL), lambda l, j: (l, 0, j)),
            pl.BlockSpec((None, 1, D_MODEL), lambda l, j: (l, 0, j)),
        ],
        out_specs=pl.BlockSpec((None, MOD_ROWS, D_MODEL), lambda l, j: (l, 0, j)),
        out_shape=jax.ShapeDtypeStruct((DEPTH, MOD_ROWS, 3 * D_MODEL), F32),
        compiler_params=_params("arbitrary", "arbitrary"),
        name="modulation",
    )(c_all, mod_w, mod_b.reshape(DEPTH, 1, 3 * D_MODEL))


def _modulated_norm(x, g, mod):
    y = x * lax.rsqrt(jnp.mean(x * x, axis=-1, keepdims=True) + EPS) * g
    return y * (1.0 + mod[:, D_MODEL:2 * D_MODEL]) + mod[:, :D_MODEL]


def _conv3(u, w, period):
    rows = u.shape[0]
    t = lax.broadcasted_iota(jnp.int32, (rows, 1), 0) & (period - 1)
    prev = jnp.where(t != 0, pltpu.roll(u, 1, axis=0), 0.0)
    nxt = jnp.where(t != period - 1, pltpu.roll(u, rows - 1, axis=0), 0.0)
    return prev * w[0:1, :] + u * w[1:2, :] + nxt * w[2:3, :]


def _sub_tiles():
    return [pl.ds(r * SUB_ROWS, SUB_ROWS) for r in range(ROW_TILE // SUB_ROWS)]


def _mod_index(tiles_per_batch):
    if tiles_per_batch is None:
        return lambda i, *_: (CTX_MOD_ROW, 0, 0)
    return lambda i, *_: (i // tiles_per_batch, 0, 0)


def _conv_layer_kernel(x_ref, mod_ref, g_ref, wb_ref, wc_ref, wu_ref, wz_ref, cw_ref, wo_ref,
                       o_ref, hx_ref, acc_ref, *, period):
    k = pl.program_id(1)

    @pl.when(k == 0)
    def _():
        for rows in _sub_tiles():
            hx_ref[rows, :] = _modulated_norm(x_ref[rows, :], g_ref[...], mod_ref[...]).astype(BF16)
        acc_ref[...] = jnp.zeros_like(acc_ref)

    for rows in _sub_tiles():
        hx = hx_ref[rows, :]
        cu = _dot(hx, wc_ref[...]) * _dot(hx, wu_ref[...])
        y = _dot(hx, wb_ref[...]) * _conv3(cu, cw_ref[...], period)
        y = y * _silu(_dot(hx, wz_ref[...]))
        acc_ref[rows, :] += _dot(y.astype(BF16), wo_ref[...])

    @pl.when(k == pl.num_programs(1) - 1)
    def _():
        for rows in _sub_tiles():
            o_ref[rows, :] = x_ref[rows, :] + mod_ref[:, 2 * D_MODEL:] * acc_ref[rows, :]


def _conv_layer(xs, mod, norm_g, w_in, conv_w, w_out, *, period, tiles_per_batch):
    rows = xs.shape[0]
    nk = D_INNER // COL_BLOCK
    row_spec = pl.BlockSpec((ROW_TILE, D_MODEL), lambda i, k: (i, 0))
    w_in_spec = lambda g: pl.BlockSpec((D_MODEL, COL_BLOCK), lambda i, k: (0, g * nk + k))
    return pl.pallas_call(
        functools.partial(_conv_layer_kernel, period=period),
        grid=(rows // ROW_TILE, nk),
        in_specs=[
            row_spec,
            pl.BlockSpec((None, 1, 3 * D_MODEL), _mod_index(tiles_per_batch)),
            pl.BlockSpec((1, D_MODEL), lambda i, k: (0, 0)),
            w_in_spec(0), w_in_spec(1), w_in_spec(2), w_in_spec(3),
            pl.BlockSpec((3, COL_BLOCK), lambda i, k: (0, k)),
            pl.BlockSpec((COL_BLOCK, D_MODEL), lambda i, k: (k, 0)),
        ],
        out_specs=row_spec,
        out_shape=jax.ShapeDtypeStruct((rows, D_MODEL), F32),
        scratch_shapes=[pltpu.VMEM((ROW_TILE, D_MODEL), BF16), pltpu.VMEM((ROW_TILE, D_MODEL), F32)],
        compiler_params=_params("parallel", "arbitrary"),
        name="conv_layer",
    )(xs, mod, norm_g, w_in, w_in, w_in, w_in, conv_w, w_out)


def _mlstm_proj_kernel(x_ref, mod_ref, g_ref, wu_ref, wz_ref, wo_ref, cw_ref, wq_ref, wk_ref, wv_ref,
                       wg_ref, bg_ref, ng_ref, sk_ref, qt_ref, k_ref, vt_ref, gates_ref, *rest, period, gated):
    hx_ref = rest[-1]
    h = pl.program_id(1)

    @pl.when(h == 0)
    def _():
        for rows in _sub_tiles():
            hx_ref[rows, :] = _modulated_norm(x_ref[rows, :], g_ref[...], mod_ref[...]).astype(BF16)
        gates_ref[...] = jnp.broadcast_to(bg_ref[...], gates_ref.shape)

    for rows in _sub_tiles():
        hx = hx_ref[rows, :]
        u = _dot(hx, wu_ref[...])
        uc = _silu(_conv3(u, cw_ref[...], period))
        ucb = uc.astype(BF16)
        q = _dot(ucb, wq_ref[...])
        k = _dot(ucb, wk_ref[...]) * (HEAD_DIM ** -0.5)
        v = _dot(u.astype(BF16), wv_ref[...])
        k_ref[rows, :] = k.astype(BF16)
        for piece in range(SUB_ROWS // SCAN_CHUNK):
            chunk = rows.start // SCAN_CHUNK + piece
            part = slice(piece * SCAN_CHUNK, (piece + 1) * SCAN_CHUNK)
            qt = q[part, :].T.astype(BF16)
            kt = k[part, :].T.astype(BF16)
            vt = v[part, :].T.astype(BF16)
            qt_ref[chunk] = qt
            vt_ref[chunk] = vt
            cols = pl.ds(chunk * SCAN_CHUNK, SCAN_CHUNK)
            gates_ref[:, cols] += _dot(wg_ref[0], qt) + _dot(wg_ref[1], kt) + _dot(wg_ref[2], vt)
        if gated:
            a_ref, b_ref = rest[:2]
            sz = _silu(_dot(hx, wz_ref[...]))
            a_ref[rows, :] = (_sigmoid(_dot(hx, wo_ref[...])) * ng_ref[...] * sz).astype(BF16)
            b_ref[rows, :] = (sk_ref[...] * uc * sz).astype(BF16)


def _mlstm_proj(xs, mod, norm_g, w_in, conv_w, wq, wk, wv, wg, bg, m_norm_g, m_skip, *, period,
                tiles_per_batch, gated):
    rows = xs.shape[0]
    w_in_spec = lambda g: pl.BlockSpec((D_MODEL, HEAD_DIM), lambda i, h: (0, g * HEADS + h))
    head_w = pl.BlockSpec((None, HEAD_DIM, HEAD_DIM), lambda i, h: (h, 0, 0))
    head_out = pl.BlockSpec((ROW_TILE, HEAD_DIM), lambda i, h: (i, h))
    head_vec = pl.BlockSpec((1, HEAD_DIM), lambda i, h: (0, h))
    inner = jax.ShapeDtypeStruct((rows, D_INNER), BF16)
    head_t = pl.BlockSpec((ROW_TILE // SCAN_CHUNK, HEAD_DIM, SCAN_CHUNK), lambda i, h: (i, h, 0))
    inner_t = jax.ShapeDtypeStruct((rows // SCAN_CHUNK, D_INNER, SCAN_CHUNK), BF16)
    n_inner = 5 if gated else 3
    outs = pl.pallas_call(
        functools.partial(_mlstm_proj_kernel, period=period, gated=gated),
        grid=(rows // ROW_TILE, HEADS),
        in_specs=[
            pl.BlockSpec((ROW_TILE, D_MODEL), lambda i, h: (i, 0)),
            pl.BlockSpec((None, 1, 3 * D_MODEL), _mod_index(tiles_per_batch)),
            pl.BlockSpec((1, D_MODEL), lambda i, h: (0, 0)),
            w_in_spec(0), w_in_spec(1), w_in_spec(2),
            pl.BlockSpec((3, HEAD_DIM), lambda i, h: (0, h)),
            head_w, head_w, head_w,
            pl.BlockSpec((3, None, N_GATES, HEAD_DIM), lambda i, h: (0, h, 0, 0)),
            pl.BlockSpec((N_GATES, 1), lambda i, h: (0, 0)),
            head_vec, head_vec,
        ],
        out_specs=[head_t, head_out, head_t, pl.BlockSpec((N_GATES, ROW_TILE), lambda i, h: (0, i))]
        + [head_out] * (n_inner - 3),
        out_shape=[inner_t, inner, inner_t, jax.ShapeDtypeStruct((N_GATES, rows), F32)]
        + [inner] * (n_inner - 3),
        scratch_shapes=[pltpu.VMEM((ROW_TILE, D_MODEL), BF16)],
        compiler_params=_params("parallel", "arbitrary"),
        name="mlstm_proj",
    )(xs, mod, norm_g, w_in, w_in, w_in, conv_w, wq, wk, wv, wg, bg, m_norm_g, m_skip)
    return outs


def _log_sigmoid(x):
    return jnp.minimum(x, 0.0) - jnp.log1p(jnp.exp(-jnp.abs(x)))


def _split3(x):
    hi = x.astype(BF16)
    r = x - hi.astype(F32)
    mid = r.astype(BF16)
    lo = (r - mid.astype(F32)).astype(BF16)
    return hi, mid, lo


def _running_max_lanes(x, reverse):
    n = x.shape[-1]
    lane = lax.broadcasted_iota(jnp.int32, x.shape, 1)
    shift = 1
    while shift < n:
        if reverse:
            moved = jnp.where(lane < n - shift, pltpu.roll(x, n - shift, axis=1), -jnp.inf)
        else:
            moved = jnp.where(lane >= shift, pltpu.roll(x, shift, axis=1), -jnp.inf)
        x = jnp.maximum(x, moved)
        shift *= 2
    return x


ROW_CR, ROW_CM, ROW_BCUM, ROW_BLAST = 0, 1, 2, 3
ROW_KINDS = 4


def _gate_prep_kernel(g_ref, crc_ref, row_ref, *, nchunks):
    L = SCAN_CHUNK
    G = 2 * HEADS
    r = lax.broadcasted_iota(jnp.int32, (L, L), 0)
    c = lax.broadcasted_iota(jnp.int32, (L, L), 1)
    lower = (c <= r).astype(BF16)
    upper = (c >= r).astype(BF16)
    fwd_row = lax.broadcasted_iota(jnp.int32, (G, 1), 0) < HEADS
    for ci in range(nchunks):
        rows = pl.ds(ci * L, L)
        g = g_ref[:, rows]
        parts = _split3(_log_sigmoid(g))
        pre = sum(_dot(p, upper) for p in parts)[G:, :]
        suf = sum(_dot(p, lower) for p in parts)[G:, :]
        bcum = jnp.where(fwd_row, pre, suf)
        cr = g[0:G, :] - bcum
        b_last = jnp.where(fwd_row, pre[:, L - 1:L], suf[:, 0:1])
        row_ref[ci, ROW_CR * G:(ROW_CR + 1) * G, :] = cr
        row_ref[ci, ROW_CM * G:(ROW_CM + 1) * G, :] = jnp.where(
            fwd_row, _running_max_lanes(cr, False), _running_max_lanes(cr, True))
        row_ref[ci, ROW_BCUM * G:(ROW_BCUM + 1) * G, :] = bcum
        row_ref[ci, ROW_BLAST * G:(ROW_BLAST + 1) * G, :] = jnp.broadcast_to(b_last, (G, L))
        crc_ref[rows, :] = jnp.concatenate([cr, jnp.zeros((GATE_LANES - G, L), F32)], axis=0).T


def _gate_prep(gates, tokens):
    nchunks = tokens // SCAN_CHUNK
    row_rows = ROW_KINDS * 2 * HEADS
    return pl.pallas_call(
        functools.partial(_gate_prep_kernel, nchunks=nchunks),
        grid=(BATCH,),
        in_specs=[pl.BlockSpec((N_GATES, tokens), lambda b: (0, b))],
        out_specs=[
            pl.BlockSpec((None, tokens, GATE_LANES), lambda b: (b, 0, 0)),
            pl.BlockSpec((None, nchunks, row_rows, SCAN_CHUNK), lambda b: (b, 0, 0, 0)),
        ],
        out_shape=[
            jax.ShapeDtypeStruct((BATCH, tokens, GATE_LANES), F32),
            jax.ShapeDtypeStruct((BATCH, nchunks, row_rows, SCAN_CHUNK), F32),
        ],
        compiler_params=_params("parallel"),
        name="gate_prep",
    )(gates)


class _Seq:
    def __init__(self, k, qt, vt, crc, row, hs, hn, nchunks):
        self.k, self.qt, self.vt, self.crc, self.row = k, qt, vt, crc, row
        self.hs, self.hn, self.nchunks = hs, hn, nchunks


def _chunk_rows(ci):
    start = ci * SCAN_CHUNK
    return pl.ds(start if isinstance(ci, int) else pl.multiple_of(start, SCAN_CHUNK), SCAN_CHUNK)


def _dot_row_halves(a, b, a2=None, b2=None):
    rows = a.shape[0]
    cut = (rows // 32) * 16
    parts = []
    for part in (slice(0, cut), slice(cut, rows)):
        acc = _dot(a[part], b)
        if a2 is not None:
            acc = acc + _dot(a2[part], b2)
        parts.append(acc)
    return jnp.concatenate(parts, axis=0)


def _direction_step(seq, ci, direction, head, ct_ref, n, m):
    L = SCAN_CHUNK
    rows = _chunk_rows(ci)
    gate = direction * HEADS + head
    row = lambda kind: seq.row[ci, pl.ds(kind * 2 * HEADS + gate, 1), :]
    k = seq.k[rows, :]
    qt = seq.qt[ci]
    vt = seq.vt[ci]
    cr_row, cm_row = row(ROW_CR), row(ROW_CM)
    top_j = jnp.maximum(m, cm_row)
    want_h = seq.hn is not None
    ct = ct_ref[...]

    if want_h:
        n_rows = jnp.broadcast_to(n.astype(BF16), (16, HEAD_DIM))
        kq = _dot_row_halves(jnp.concatenate([k, n_rows, ct.astype(BF16)], axis=0), qt)

    top = jnp.maximum(m, jnp.max(cr_row, axis=-1, keepdims=True))
    w_decay = jnp.exp(m - top)
    ws = jnp.exp(cr_row - top).astype(BF16)
    upd = _dot(jnp.concatenate([vt * ws, jnp.broadcast_to(ws, (16, L))], axis=0), k)

    h = None
    if want_h:
        lane_sel = lax.broadcasted_iota(jnp.int32, (1, GATE_LANES), 1) == gate
        cr_col = jnp.sum(jnp.where(lane_sel, seq.crc[rows, :], 0.0), axis=-1, keepdims=True)
        s = lax.broadcasted_iota(jnp.int32, (L, L), 0)
        j = lax.broadcasted_iota(jnp.int32, (L, L), 1)
        seen = (s <= j) if direction == 0 else (s >= j)
        decay = jnp.exp(jnp.where(seen, cr_col - cm_row, -jnp.inf))
        sc = kq[:L] * decay
        w_intra = jnp.exp(cm_row - top_j)
        w_inter = jnp.exp(m - top_j)
        den = w_inter * kq[L:L + 1] + w_intra * jnp.sum(sc, axis=0, keepdims=True)
        inv = 1.0 / jnp.maximum(jnp.abs(den), jnp.exp(-(row(ROW_BCUM) + top_j)))
        ss = (sc * (w_intra * inv)).astype(BF16)
        h = (w_inter * inv) * kq[L + 16:] + _dot_row_halves(vt, ss)

    ct_ref[...] = w_decay * ct + upd[:HEAD_DIM]
    n_new = w_decay * n + upd[HEAD_DIM:HEAD_DIM + 1]
    m_new = row(ROW_BLAST)[:, 0:1] + top
    return h, n_new, m_new


def _head_norm_t(hsum):
    hc = hsum - jnp.mean(hsum, axis=0, keepdims=True)
    var = jnp.mean(hc * hc, axis=0, keepdims=True)
    return (hc * lax.rsqrt(var + EPS)).T.astype(BF16)


def _scan_stream(seq, direction, head, ct_ref, state):
    nch = seq.nchunks

    def body(i, st):
        ci = i if direction == 0 else nch - 1 - i
        h, n, m = _direction_step(seq, ci, direction, head, ct_ref, *st)
        if seq.hn is not None:
            if direction == 0:
                seq.hs[ci] = h
            else:
                seq.hn[_chunk_rows(ci), :] = _head_norm_t(seq.hs[ci] + h)
        return n, m

    for i in range(nch):
        state = body(i, state)
    return state


def _scan_kernel(*refs, ctx_out):
    kx, qtx, vtx, crcx, rowx, kc, qtc, vtc, crcc, rowc = refs[:10]
    refs = list(refs[10:])
    hnx = refs.pop(0)
    hnc = refs.pop(0) if ctx_out else None
    ct_ref, hsx = refs[:2]
    hsc = refs[2] if ctx_out else None
    head = pl.program_id(1)
    ctx = _Seq(kc, qtc, vtc, crcc, rowc, hsc, hnc, CTX_LEN // SCAN_CHUNK)
    lat = _Seq(kx, qtx, vtx, crcx, rowx, hsx, hnx, SEQ // SCAN_CHUNK)
    for direction in (0, 1):
        ct_ref[...] = jnp.zeros_like(ct_ref)
        state = (jnp.zeros((1, HEAD_DIM), F32), jnp.zeros((1, 1), F32))
        state = _scan_stream(ctx, direction, head, ct_ref, state)
        _scan_stream(lat, direction, head, ct_ref, state)


def _scan(lat, ctx, *, ctx_out):
    def specs(tokens):
        nch = tokens // SCAN_CHUNK
        qv_t = pl.BlockSpec((None, nch, HEAD_DIM, SCAN_CHUNK), lambda b, h: (b, 0, h, 0))
        return [pl.BlockSpec((None, tokens, HEAD_DIM), lambda b, h: (b, 0, h)), qv_t, qv_t,
                pl.BlockSpec((None, tokens, GATE_LANES), lambda b, h: (b, 0, 0)),
                pl.BlockSpec((None, nch, ROW_KINDS * 2 * HEADS, SCAN_CHUNK), lambda b, h: (b, 0, 0, 0))]

    hn_spec = lambda tokens: pl.BlockSpec((None, tokens, HEAD_DIM), lambda b, h: (b, 0, h))
    hn_shape = lambda tokens: jax.ShapeDtypeStruct((BATCH, tokens, D_INNER), BF16)
    hs_shape = lambda tokens: pltpu.VMEM((tokens // SCAN_CHUNK, HEAD_DIM, SCAN_CHUNK), F32)
    out_specs = [hn_spec(SEQ)] + ([hn_spec(CTX_LEN)] if ctx_out else [])
    out_shape = [hn_shape(SEQ)] + ([hn_shape(CTX_LEN)] if ctx_out else [])
    scratch = [pltpu.VMEM((HEAD_DIM, HEAD_DIM), F32), hs_shape(SEQ)] + ([hs_shape(CTX_LEN)] if ctx_out else [])
    outs = pl.pallas_call(
        functools.partial(_scan_kernel, ctx_out=ctx_out),
        grid=(BATCH, HEADS),
        in_specs=specs(SEQ) + specs(CTX_LEN),
        out_specs=out_specs,
        out_shape=out_shape,
        scratch_shapes=scratch,
        compiler_params=_params("parallel", "arbitrary"),
        name="mlstm_scan",
    )(*lat, *ctx)
    return outs if ctx_out else (outs[0], None)


def _mlstm_out_kernel(x_ref, mod_ref, hn_ref, a_ref, b_ref, wo_ref, fg_ref, out_ref, *, final):
    for r in range(OUT_ROWS // OUT_SUB_ROWS):
        rows = pl.ds(r * OUT_SUB_ROWS, OUT_SUB_ROWS)
        y = hn_ref[rows, :].astype(F32) * a_ref[rows, :].astype(F32) + b_ref[rows, :].astype(F32)
        xn = x_ref[rows, :] + mod_ref[:, 2 * D_MODEL:] * _dot(y.astype(BF16), wo_ref[...])
        if final:
            xn = xn * lax.rsqrt(jnp.mean(xn * xn, axis=-1, keepdims=True) + EPS) * fg_ref[...]
        out_ref[rows, :] = xn


def _mlstm_out(xs, mod, hn, a, b, w_out, final_g, *, tiles_per_batch, final):
    rows = xs.shape[0]
    row_spec = pl.BlockSpec((OUT_ROWS, D_MODEL), lambda i: (i, 0))
    inner = pl.BlockSpec((OUT_ROWS, D_INNER), lambda i: (i, 0))
    return pl.pallas_call(
        functools.partial(_mlstm_out_kernel, final=final),
        grid=(rows // OUT_ROWS,),
        in_specs=[
            row_spec,
            pl.BlockSpec((None, 1, 3 * D_MODEL), _mod_index(tiles_per_batch)),
            inner, inner, inner,
            pl.BlockSpec((D_INNER, D_MODEL), lambda i: (0, 0)),
            pl.BlockSpec((1, D_MODEL), lambda i: (0, 0)),
        ],
        out_specs=row_spec,
        out_shape=jax.ShapeDtypeStruct((rows, D_MODEL), F32),
        compiler_params=_params("parallel"),
        name="mlstm_out",
    )(xs, mod, hn, a, b, w_out, final_g)


def _gate_weights(w_gate, b_gate):
    def order(a):
        a = a.reshape(a.shape[:-1] + (4, HEADS))
        return jnp.concatenate([a[..., 0::2, :], a[..., 1::2, :]], axis=-2).reshape(a.shape[:-2] + (N_GATES,))

    wg = jnp.swapaxes(order(w_gate).reshape(3, HEADS, HEAD_DIM, N_GATES), -1, -2).astype(BF16)
    return wg, order(b_gate).reshape(N_GATES, 1)


def kernel(x, c, ctx, c_ctx, norm_g, mod_w, mod_b, conv_w_in, conv_w, conv_w_out, m_w_in, m_conv_w, m_wq,
           m_wk, m_wv, m_w_gate, m_b_gate, m_norm_g, m_skip, m_w_out, final_g):
    assert x.shape == (BATCH, SEQ, D_MODEL) and ctx.shape == (BATCH, CTX_LEN, D_MODEL)
    c_all = jnp.zeros((MOD_ROWS, D_MODEL), F32).at[:BATCH].set(c).at[CTX_MOD_ROW].set(c_ctx)
    mods = _modulation(c_all, mod_w, mod_b).reshape(DEPTH, MOD_ROWS, 1, 3 * D_MODEL)
    xs = x.reshape(BATCH * SEQ, D_MODEL)
    cs = ctx.reshape(BATCH * CTX_LEN, D_MODEL)
    lat_tiles = SEQ // ROW_TILE
    final_g2 = final_g.reshape(1, D_MODEL)
    for i in range(DEPTH):
        last = i == DEPTH - 1
        j = i // 2
        g = norm_g[i].reshape(1, D_MODEL)
        if i % 2 == 0:
            w_in, w_out = conv_w_in[j].astype(BF16), conv_w_out[j].astype(BF16)
            xs = _conv_layer(xs, mods[i], g, w_in, conv_w[j], w_out, period=GRID_W, tiles_per_batch=lat_tiles)
            if not last:
                cs = _conv_layer(cs, mods[i], g, w_in, conv_w[j], w_out, period=CTX_LEN, tiles_per_batch=None)
        else:
            w_in, w_out = m_w_in[j].astype(BF16), m_w_out[j].astype(BF16)
            wq, wk, wv = m_wq[j].astype(BF16), m_wk[j].astype(BF16), m_wv[j].astype(BF16)
            wg, bg = _gate_weights(m_w_gate[j], m_b_gate[j])
            proj = functools.partial(_mlstm_proj, norm_g=g, w_in=w_in, conv_w=m_conv_w[j], wq=wq, wk=wk, wv=wv,
                                     wg=wg, bg=bg, m_norm_g=m_norm_g[j].reshape(1, D_INNER),
                                     m_skip=m_skip[j].reshape(1, D_INNER))
            px = proj(xs, mods[i], period=GRID_W, tiles_per_batch=lat_tiles, gated=True)
            pc = proj(cs, mods[i], period=CTX_LEN, tiles_per_batch=None, gated=not last)

            def stream(p, tokens):
                qt, k, vt, gates = p[:4]
                t_shape = (BATCH, tokens // SCAN_CHUNK, D_INNER, SCAN_CHUNK)
                return [k.reshape(BATCH, tokens, D_INNER), qt.reshape(t_shape), vt.reshape(t_shape)] + list(
                    _gate_prep(gates, tokens))

            hnx, hnc = _scan(stream(px, SEQ), stream(pc, CTX_LEN), ctx_out=not last)
            out = functools.partial(_mlstm_out, w_out=w_out, final_g=final_g2)
            if not last:
                cs = out(cs, mods[i], hnc.reshape(-1, D_INNER), *pc[4:], tiles_per_batch=None, final=False)
            xs = out(xs, mods[i], hnx.reshape(-1, D_INNER), *px[4:], tiles_per_batch=SEQ // OUT_ROWS, final=last)
    return xs.reshape(BATCH, SEQ, D_MODEL)
```

```python
import functools

import jax
import jax.numpy as jnp
from jax import lax
from jax.experimental import pallas as pl
from jax.experimental.pallas import tpu as pltpu

D_MODEL = 1024
BATCH = 16
SEQ = 2048
DEPTH = 4
GRID_W = 64
CTX_LEN = 256
D_INNER = 2 * D_MODEL
HEADS = 4
HEAD_DIM = D_INNER // HEADS
EPS = 1e-6

SCAN_CHUNK = 256
COL_BLOCK = 512
ROW_TILE = 1024
SUB_ROWS = 512
OUT_ROWS = 512
OUT_SUB_ROWS = 256
MOD_ROWS = 24
CTX_MOD_ROW = BATCH
GATE_LANES = 128
N_GATES = 4 * HEADS
VMEM_LIMIT_BYTES = 56 * 1024 * 1024

F32 = jnp.float32
BF16 = jnp.bfloat16


def _dot(a, b):
    return jnp.dot(a, b, preferred_element_type=F32)


def _sigmoid(x):
    return 0.5 * jnp.tanh(0.5 * x) + 0.5


def _silu(x):
    return x * _sigmoid(x)


def _params(*sem):
    return pltpu.CompilerParams(dimension_semantics=sem, vmem_limit_bytes=VMEM_LIMIT_BYTES)


def _mod_kernel(c_ref, w_ref, b_ref, o_ref):
    sc = _silu(c_ref[...])
    o_ref[...] = jnp.dot(sc, w_ref[...], preferred_element_type=F32,
                         precision=lax.Precision.HIGHEST) + b_ref[...]


def _modulation(c_all, mod_w, mod_b):
    nblk = 3 * D_MODEL // D_MODEL
    return pl.pallas_call(
        _mod_kernel,
        grid=(DEPTH, nblk),
        in_specs=[
            pl.BlockSpec((MOD_ROWS, D_MODEL), lambda l, j: (0, 0)),
            pl.BlockSpec((None, D_MODEL, D_MODEL), lambda l, j: (l, 0, j)),
            pl.BlockSpec((None, 1, D_MODEL), lambda l, j: (l, 0, j)),
        ],
        out_specs=pl.BlockSpec((None, MOD_ROWS, D_MODEL), lambda l, j: (l, 0, j)),
        out_shape=jax.ShapeDtypeStruct((DEPTH, MOD_ROWS, 3 * D_MODEL), F32),
        compiler_params=_params("arbitrary", "arbitrary"),
        name="modulation",
    )(c_all, mod_w, mod_b.reshape(DEPTH, 1, 3 * D_MODEL))


def _modulated_norm(x, g, mod):
    y = x * lax.rsqrt(jnp.mean(x * x, axis=-1, keepdims=True) + EPS) * g
    return y * (1.0 + mod[:, D_MODEL:2 * D_MODEL]) + mod[:, :D_MODEL]


def _conv3(u, w, period):
    rows = u.shape[0]
    t = lax.broadcasted_iota(jnp.int32, (rows, 1), 0) & (period - 1)
    prev = jnp.where(t != 0, pltpu.roll(u, 1, axis=0), 0.0)
    nxt = jnp.where(t != period - 1, pltpu.roll(u, rows - 1, axis=0), 0.0)
    return prev * w[0:1, :] + u * w[1:2, :] + nxt * w[2:3, :]


def _sub_tiles():
    return [pl.ds(r * SUB_ROWS, SUB_ROWS) for r in range(ROW_TILE // SUB_ROWS)]


def _mod_index(tiles_per_batch):
    if tiles_per_batch is None:
        return lambda i, *_: (CTX_MOD_ROW, 0, 0)
    return lambda i, *_: (i // tiles_per_batch, 0, 0)


def _conv_layer_kernel(x_ref, mod_ref, g_ref, wb_ref, wc_ref, wu_ref, wz_ref, cw_ref, wo_ref,
                       o_ref, hx_ref, acc_ref, *, period):
    k = pl.program_id(1)

    @pl.when(k == 0)
    def _():
        for rows in _sub_tiles():
            hx_ref[rows, :] = _modulated_norm(x_ref[rows, :], g_ref[...], mod_ref[...]).astype(BF16)
        acc_ref[...] = jnp.zeros_like(acc_ref)

    for rows in _sub_tiles():
        hx = hx_ref[rows, :]
        cu = _dot(hx, wc_ref[...]) * _dot(hx, wu_ref[...])
        y = _dot(hx, wb_ref[...]) * _conv3(cu, cw_ref[...], period)
        y = y * _silu(_dot(hx, wz_ref[...]))
        acc_ref[rows, :] += _dot(y.astype(BF16), wo_ref[...])

    @pl.when(k == pl.num_programs(1) - 1)
    def _():
        for rows in _sub_tiles():
            o_ref[rows, :] = x_ref[rows, :] + mod_ref[:, 2 * D_MODEL:] * acc_ref[rows, :]


def _conv_layer(xs, mod, norm_g, w_in, conv_w, w_out, *, period, tiles_per_batch):
    rows = xs.shape[0]
    nk = D_INNER // COL_BLOCK
    row_spec = pl.BlockSpec((ROW_TILE, D_MODEL), lambda i, k: (i, 0))
    w_in_spec = lambda g: pl.BlockSpec((D_MODEL, COL_BLOCK), lambda i, k: (0, g * nk + k))
    return pl.pallas_call(
        functools.partial(_conv_layer_kernel, period=period),
        grid=(rows // ROW_TILE, nk),
        in_specs=[
            row_spec,
            pl.BlockSpec((None, 1, 3 * D_MODEL), _mod_index(tiles_per_batch)),
            pl.BlockSpec((1, D_MODEL), lambda i, k: (0, 0)),
            w_in_spec(0), w_in_spec(1), w_in_spec(2), w_in_spec(3),
            pl.BlockSpec((3, COL_BLOCK), lambda i, k: (0, k)),
            pl.BlockSpec((COL_BLOCK, D_MODEL), lambda i, k: (k, 0)),
        ],
        out_specs=row_spec,
        out_shape=jax.ShapeDtypeStruct((rows, D_MODEL), F32),
        scratch_shapes=[pltpu.VMEM((ROW_TILE, D_MODEL), BF16), pltpu.VMEM((ROW_TILE, D_MODEL), F32)],
        compiler_params=_params("parallel", "arbitrary"),
        name="conv_layer",
    )(xs, mod, norm_g, w_in, w_in, w_in, w_in, conv_w, w_out)


def _mlstm_proj_kernel(x_ref, mod_ref, g_ref, wu_ref, wz_ref, wo_ref, cw_ref, wq_ref, wk_ref, wv_ref,
                       wg_ref, bg_ref, ng_ref, sk_ref, qt_ref, k_ref, vt_ref, gates_ref, *rest, period, gated):
    hx_ref = rest[-1]
    h = pl.program_id(1)

    @pl.when(h == 0)
    def _():
        for rows in _sub_tiles():
            hx_ref[rows, :] = _modulated_norm(x_ref[rows, :], g_ref[...], mod_ref[...]).astype(BF16)
        gates_ref[...] = jnp.broadcast_to(bg_ref[...], gates_ref.shape)

    subs = _sub_tiles()
    us = [_dot(hx_ref[rows, :], wu_ref[...]) for rows in subs]
    if gated:
        zs = [_dot(hx_ref[rows, :], wz_ref[...]) for rows in subs]
        os_ = [_dot(hx_ref[rows, :], wo_ref[...]) for rows in subs]
    ucs = [_silu(_conv3(u, cw_ref[...], period)) for u in us]

    qkv = []
    for u, uc in zip(us, ucs):
        ucb = uc.astype(BF16)
        qkv.append((_dot(ucb, wq_ref[...]), _dot(ucb, wk_ref[...]) * (HEAD_DIM ** -0.5),
                    _dot(u.astype(BF16), wv_ref[...])))

    for idx, rows in enumerate(subs):
        uc = ucs[idx]
        q, k, v = qkv[idx]
        k_ref[rows, :] = k.astype(BF16)
        for piece in range(SUB_ROWS // SCAN_CHUNK):
            chunk = rows.start // SCAN_CHUNK + piece
            part = slice(piece * SCAN_CHUNK, (piece + 1) * SCAN_CHUNK)
            qt = q[part, :].T.astype(BF16)
            kt = k[part, :].T.astype(BF16)
            vt = v[part, :].T.astype(BF16)
            qt_ref[chunk] = qt
            vt_ref[chunk] = vt
            cols = pl.ds(chunk * SCAN_CHUNK, SCAN_CHUNK)
            gates_ref[:, cols] += _dot(wg_ref[0], qt) + _dot(wg_ref[1], kt) + _dot(wg_ref[2], vt)
        if gated:
            a_ref, b_ref = rest[:2]
            sz = _silu(zs[idx])
            a_ref[rows, :] = (_sigmoid(os_[idx]) * ng_ref[...] * sz).astype(BF16)
            b_ref[rows, :] = (sk_ref[...] * uc * sz).astype(BF16)


def _mlstm_proj(xs, mod, norm_g, w_in, conv_w, wq, wk, wv, wg, bg, m_norm_g, m_skip, *, period,
                tiles_per_batch, gated):
    rows = xs.shape[0]
    w_in_spec = lambda g: pl.BlockSpec((D_MODEL, HEAD_DIM), lambda i, h: (0, g * HEADS + h))
    head_w = pl.BlockSpec((None, HEAD_DIM, HEAD_DIM), lambda i, h: (h, 0, 0))
    head_out = pl.BlockSpec((ROW_TILE, HEAD_DIM), lambda i, h: (i, h))
    head_vec = pl.BlockSpec((1, HEAD_DIM), lambda i, h: (0, h))
    inner = jax.ShapeDtypeStruct((rows, D_INNER), BF16)
    head_t = pl.BlockSpec((ROW_TILE // SCAN_CHUNK, HEAD_DIM, SCAN_CHUNK), lambda i, h: (i, h, 0))
    inner_t = jax.ShapeDtypeStruct((rows // SCAN_CHUNK, D_INNER, SCAN_CHUNK), BF16)
    n_inner = 5 if gated else 3
    outs = pl.pallas_call(
        functools.partial(_mlstm_proj_kernel, period=period, gated=gated),
        grid=(rows // ROW_TILE, HEADS),
        in_specs=[
            pl.BlockSpec((ROW_TILE, D_MODEL), lambda i, h: (i, 0)),
            pl.BlockSpec((None, 1, 3 * D_MODEL), _mod_index(tiles_per_batch)),
            pl.BlockSpec((1, D_MODEL), lambda i, h: (0, 0)),
            w_in_spec(0), w_in_spec(1), w_in_spec(2),
            pl.BlockSpec((3, HEAD_DIM), lambda i, h: (0, h)),
            head_w, head_w, head_w,
            pl.BlockSpec((3, None, N_GATES, HEAD_DIM), lambda i, h: (0, h, 0, 0)),
            pl.BlockSpec((N_GATES, 1), lambda i, h: (0, 0)),
            head_vec, head_vec,
        ],
        out_specs=[head_t, head_out, head_t, pl.BlockSpec((N_GATES, ROW_TILE), lambda i, h: (0, i))]
        + [head_out] * (n_inner - 3),
        out_shape=[inner_t, inner, inner_t, jax.ShapeDtypeStruct((N_GATES, rows), F32)]
        + [inner] * (n_inner - 3),
        scratch_shapes=[pltpu.VMEM((ROW_TILE, D_MODEL), BF16)],
        compiler_params=_params("parallel", "arbitrary"),
        name="mlstm_proj",
    )(xs, mod, norm_g, w_in, w_in, w_in, conv_w, wq, wk, wv, wg, bg, m_norm_g, m_skip)
    return outs


def _log_sigmoid(x):
    return jnp.minimum(x, 0.0) - jnp.log1p(jnp.exp(-jnp.abs(x)))


def _split3(x):
    hi = x.astype(BF16)
    r = x - hi.astype(F32)
    mid = r.astype(BF16)
    lo = (r - mid.astype(F32)).astype(BF16)
    return hi, mid, lo


def _running_max_lanes(x, reverse):
    n = x.shape[-1]
    lane = lax.broadcasted_iota(jnp.int32, x.shape, 1)
    shift = 1
    while shift < n:
        if reverse:
            moved = jnp.where(lane < n - shift, pltpu.roll(x, n - shift, axis=1), -jnp.inf)
        else:
            moved = jnp.where(lane >= shift, pltpu.roll(x, shift, axis=1), -jnp.inf)
        x = jnp.maximum(x, moved)
        shift *= 2
    return x


ROW_CR, ROW_CM, ROW_BCUM, ROW_BLAST = 0, 1, 2, 3
ROW_KINDS = 4


def _gate_prep_kernel(g_ref, crc_ref, row_ref, *, nchunks):
    L = SCAN_CHUNK
    G = 2 * HEADS
    r = lax.broadcasted_iota(jnp.int32, (L, L), 0)
    c = lax.broadcasted_iota(jnp.int32, (L, L), 1)
    lower = (c <= r).astype(BF16)
    upper = (c >= r).astype(BF16)
    fwd_row = lax.broadcasted_iota(jnp.int32, (G, 1), 0) < HEADS
    for ci in range(nchunks):
        rows = pl.ds(ci * L, L)
        g = g_ref[:, rows]
        parts = _split3(_log_sigmoid(g))
        pre = sum(_dot(p, upper) for p in parts)[G:, :]
        suf = sum(_dot(p, lower) for p in parts)[G:, :]
        bcum = jnp.where(fwd_row, pre, suf)
        cr = g[0:G, :] - bcum
        b_last = jnp.where(fwd_row, pre[:, L - 1:L], suf[:, 0:1])
        row_ref[ci, ROW_CR * G:(ROW_CR + 1) * G, :] = cr
        row_ref[ci, ROW_CM * G:(ROW_CM + 1) * G, :] = jnp.where(
            fwd_row, _running_max_lanes(cr, False), _running_max_lanes(cr, True))
        row_ref[ci, ROW_BCUM * G:(ROW_BCUM + 1) * G, :] = bcum
        row_ref[ci, ROW_BLAST * G:(ROW_BLAST + 1) * G, :] = jnp.broadcast_to(b_last, (G, L))
        crc_ref[rows, :] = jnp.concatenate([cr, jnp.zeros((GATE_LANES - G, L), F32)], axis=0).T


def _gate_prep(gates, tokens):
    nchunks = tokens // SCAN_CHUNK
    row_rows = ROW_KINDS * 2 * HEADS
    return pl.pallas_call(
        functools.partial(_gate_prep_kernel, nchunks=nchunks),
        grid=(BATCH,),
        in_specs=[pl.BlockSpec((N_GATES, tokens), lambda b: (0, b))],
        out_specs=[
            pl.BlockSpec((None, tokens, GATE_LANES), lambda b: (b, 0, 0)),
            pl.BlockSpec((None, nchunks, row_rows, SCAN_CHUNK), lambda b: (b, 0, 0, 0)),
        ],
        out_shape=[
            jax.ShapeDtypeStruct((BATCH, tokens, GATE_LANES), F32),
            jax.ShapeDtypeStruct((BATCH, nchunks, row_rows, SCAN_CHUNK), F32),
        ],
        compiler_params=_params("parallel"),
        name="gate_prep",
    )(gates)


class _Seq:
    def __init__(self, k, qt, vt, crc, row, hs, hn, nchunks):
        self.k, self.qt, self.vt, self.crc, self.row = k, qt, vt, crc, row
        self.hs, self.hn, self.nchunks = hs, hn, nchunks


def _chunk_rows(ci):
    start = ci * SCAN_CHUNK
    return pl.ds(start if isinstance(ci, int) else pl.multiple_of(start, SCAN_CHUNK), SCAN_CHUNK)


def _dot_row_halves(a, b, a2=None, b2=None):
    rows = a.shape[0]
    cut = (rows // 32) * 16
    parts = []
    for part in (slice(0, cut), slice(cut, rows)):
        acc = _dot(a[part], b)
        if a2 is not None:
            acc = acc + _dot(a2[part], b2)
        parts.append(acc)
    return jnp.concatenate(parts, axis=0)


def _direction_step(seq, ci, direction, head, ct_ref, n, m):
    L = SCAN_CHUNK
    rows = _chunk_rows(ci)
    gate = direction * HEADS + head
    row = lambda kind: seq.row[ci, pl.ds(kind * 2 * HEADS + gate, 1), :]
    k = seq.k[rows, :]
    qt = seq.qt[ci]
    vt = seq.vt[ci]
    cr_row, cm_row = row(ROW_CR), row(ROW_CM)
    top_j = jnp.maximum(m, cm_row)
    want_h = seq.hn is not None
    ct = ct_ref[...]

    if want_h:
        n_rows = jnp.broadcast_to(n.astype(BF16), (16, HEAD_DIM))
        kq = _dot_row_halves(jnp.concatenate([k, n_rows, ct.astype(BF16)], axis=0), qt)

    top = jnp.maximum(m, jnp.max(cr_row, axis=-1, keepdims=True))
    w_decay = jnp.exp(m - top)
    ws = jnp.exp(cr_row - top).astype(BF16)
    upd = _dot(jnp.concatenate([vt * ws, jnp.broadcast_to(ws, (16, L))], axis=0), k)

    h = None
    if want_h:
        lane_sel = lax.broadcasted_iota(jnp.int32, (1, GATE_LANES), 1) == gate
        cr_col = jnp.sum(jnp.where(lane_sel, seq.crc[rows, :], 0.0), axis=-1, keepdims=True)
        s = lax.broadcasted_iota(jnp.int32, (L, L), 0)
        j = lax.broadcasted_iota(jnp.int32, (L, L), 1)
        seen = (s <= j) if direction == 0 else (s >= j)
        decay = jnp.exp(jnp.where(seen, cr_col - cm_row, -jnp.inf))
        sc = kq[:L] * decay
        w_intra = jnp.exp(cm_row - top_j)
        w_inter = jnp.exp(m - top_j)
        den = w_inter * kq[L:L + 1] + w_intra * jnp.sum(sc, axis=0, keepdims=True)
        inv = 1.0 / jnp.maximum(jnp.abs(den), jnp.exp(-(row(ROW_BCUM) + top_j)))
        ss = (sc * (w_intra * inv)).astype(BF16)
        h = (w_inter * inv) * kq[L + 16:] + _dot_row_halves(vt, ss)

    ct_ref[...] = w_decay * ct + upd[:HEAD_DIM]
    n_new = w_decay * n + upd[HEAD_DIM:HEAD_DIM + 1]
    m_new = row(ROW_BLAST)[:, 0:1] + top
    return h, n_new, m_new


def _head_norm_t(hsum):
    hc = hsum - jnp.mean(hsum, axis=0, keepdims=True)
    var = jnp.mean(hc * hc, axis=0, keepdims=True)
    return (hc * lax.rsqrt(var + EPS)).T.astype(BF16)


def _scan_stream(seq, direction, head, ct_ref, state):
    nch = seq.nchunks

    def body(i, st):
        ci = i if direction == 0 else nch - 1 - i
        h, n, m = _direction_step(seq, ci, direction, head, ct_ref, *st)
        if seq.hn is not None:
            if direction == 0:
                seq.hs[ci] = h
            else:
                seq.hn[_chunk_rows(ci), :] = _head_norm_t(seq.hs[ci] + h)
        return n, m

    for i in range(nch):
        state = body(i, state)
    return state


def _scan_kernel(*refs, ctx_out):
    kx, qtx, vtx, crcx, rowx, kc, qtc, vtc, crcc, rowc = refs[:10]
    refs = list(refs[10:])
    hnx = refs.pop(0)
    hnc = refs.pop(0) if ctx_out else None
    ct_ref, hsx = refs[:2]
    hsc = refs[2] if ctx_out else None
    head = pl.program_id(1)
    ctx = _Seq(kc, qtc, vtc, crcc, rowc, hsc, hnc, CTX_LEN // SCAN_CHUNK)
    lat = _Seq(kx, qtx, vtx, crcx, rowx, hsx, hnx, SEQ // SCAN_CHUNK)
    for direction in (0, 1):
        ct_ref[...] = jnp.zeros_like(ct_ref)
        state = (jnp.zeros((1, HEAD_DIM), F32), jnp.zeros((1, 1), F32))
        state = _scan_stream(ctx, direction, head, ct_ref, state)
        _scan_stream(lat, direction, head, ct_ref, state)


def _scan(lat, ctx, *, ctx_out):
    def specs(tokens):
        nch = tokens // SCAN_CHUNK
        qv_t = pl.BlockSpec((None, nch, HEAD_DIM, SCAN_CHUNK), lambda b, h: (b, 0, h, 0))
        return [pl.BlockSpec((None, tokens, HEAD_DIM), lambda b, h: (b, 0, h)), qv_t, qv_t,
                pl.BlockSpec((None, tokens, GATE_LANES), lambda b, h: (b, 0, 0)),
                pl.BlockSpec((None, nch, ROW_KINDS * 2 * HEADS, SCAN_CHUNK), lambda b, h: (b, 0, 0, 0))]

    hn_spec = lambda tokens: pl.BlockSpec((None, tokens, HEAD_DIM), lambda b, h: (b, 0, h))
    hn_shape = lambda tokens: jax.ShapeDtypeStruct((BATCH, tokens, D_INNER), BF16)
    hs_shape = lambda tokens: pltpu.VMEM((tokens // SCAN_CHUNK, HEAD_DIM, SCAN_CHUNK), F32)
    out_specs = [hn_spec(SEQ)] + ([hn_spec(CTX_LEN)] if ctx_out else [])
    out_shape = [hn_shape(SEQ)] + ([hn_shape(CTX_LEN)] if ctx_out else [])
    scratch = [pltpu.VMEM((HEAD_DIM, HEAD_DIM), F32), hs_shape(SEQ)] + ([hs_shape(CTX_LEN)] if ctx_out else [])
    outs = pl.pallas_call(
        functools.partial(_scan_kernel, ctx_out=ctx_out),
        grid=(BATCH, HEADS),
        in_specs=specs(SEQ) + specs(CTX_LEN),
        out_specs=out_specs,
        out_shape=out_shape,
        scratch_shapes=scratch,
        compiler_params=_params("parallel", "arbitrary"),
        name="mlstm_scan",
    )(*lat, *ctx)
    return outs if ctx_out else (outs[0], None)


def _mlstm_out_kernel(x_ref, mod_ref, hn_ref, a_ref, b_ref, wo_ref, fg_ref, out_ref, *, final):
    for r in range(OUT_ROWS // OUT_SUB_ROWS):
        rows = pl.ds(r * OUT_SUB_ROWS, OUT_SUB_ROWS)
        y = hn_ref[rows, :].astype(F32) * a_ref[rows, :].astype(F32) + b_ref[rows, :].astype(F32)
        xn = x_ref[rows, :] + mod_ref[:, 2 * D_MODEL:] * _dot(y.astype(BF16), wo_ref[...])
        if final:
            xn = xn * lax.rsqrt(jnp.mean(xn * xn, axis=-1, keepdims=True) + EPS) * fg_ref[...]
        out_ref[rows, :] = xn


def _mlstm_out(xs, mod, hn, a, b, w_out, final_g, *, tiles_per_batch, final):
    rows = xs.shape[0]
    row_spec = pl.BlockSpec((OUT_ROWS, D_MODEL), lambda i: (i, 0))
    inner = pl.BlockSpec((OUT_ROWS, D_INNER), lambda i: (i, 0))
    return pl.pallas_call(
        functools.partial(_mlstm_out_kernel, final=final),
        grid=(rows // OUT_ROWS,),
        in_specs=[
            row_spec,
            pl.BlockSpec((None, 1, 3 * D_MODEL), _mod_index(tiles_per_batch)),
            inner, inner, inner,
            pl.BlockSpec((D_INNER, D_MODEL), lambda i: (0, 0)),
            pl.BlockSpec((1, D_MODEL), lambda i: (0, 0)),
        ],
        out_specs=row_spec,
        out_shape=jax.ShapeDtypeStruct((rows, D_MODEL), F32),
        compiler_params=_params("parallel"),
        name="mlstm_out",
    )(xs, mod, hn, a, b, w_out, final_g)


def _gate_weights(w_gate, b_gate):
    def order(a):
        a = a.reshape(a.shape[:-1] + (4, HEADS))
        return jnp.concatenate([a[..., 0::2, :], a[..., 1::2, :]], axis=-2).reshape(a.shape[:-2] + (N_GATES,))

    wg = jnp.swapaxes(order(w_gate).reshape(3, HEADS, HEAD_DIM, N_GATES), -1, -2).astype(BF16)
    return wg, order(b_gate).reshape(N_GATES, 1)


def kernel(x, c, ctx, c_ctx, norm_g, mod_w, mod_b, conv_w_in, conv_w, conv_w_out, m_w_in, m_conv_w, m_wq,
           m_wk, m_wv, m_w_gate, m_b_gate, m_norm_g, m_skip, m_w_out, final_g):
    assert x.shape == (BATCH, SEQ, D_MODEL) and ctx.shape == (BATCH, CTX_LEN, D_MODEL)
    c_all = jnp.zeros((MOD_ROWS, D_MODEL), F32).at[:BATCH].set(c).at[CTX_MOD_ROW].set(c_ctx)
    mods = _modulation(c_all, mod_w, mod_b).reshape(DEPTH, MOD_ROWS, 1, 3 * D_MODEL)
    xs = x.reshape(BATCH * SEQ, D_MODEL)
    cs = ctx.reshape(BATCH * CTX_LEN, D_MODEL)
    lat_tiles = SEQ // ROW_TILE
    final_g2 = final_g.reshape(1, D_MODEL)
    for i in range(DEPTH):
        last = i == DEPTH - 1
        j = i // 2
        g = norm_g[i].reshape(1, D_MODEL)
        if i % 2 == 0:
            w_in, w_out = conv_w_in[j].astype(BF16), conv_w_out[j].astype(BF16)
            xs = _conv_layer(xs, mods[i], g, w_in, conv_w[j], w_out, period=GRID_W, tiles_per_batch=lat_tiles)
            if not last:
                cs = _conv_layer(cs, mods[i], g, w_in, conv_w[j], w_out, period=CTX_LEN, tiles_per_batch=None)
        else:
            w_in, w_out = m_w_in[j].astype(BF16), m_w_out[j].astype(BF16)
            wq, wk, wv = m_wq[j].astype(BF16), m_wk[j].astype(BF16), m_wv[j].astype(BF16)
            wg, bg = _gate_weights(m_w_gate[j], m_b_gate[j])
            proj = functools.partial(_mlstm_proj, norm_g=g, w_in=w_in, conv_w=m_conv_w[j], wq=wq, wk=wk, wv=wv,
                                     wg=wg, bg=bg, m_norm_g=m_norm_g[j].reshape(1, D_INNER),
                                     m_skip=m_skip[j].reshape(1, D_INNER))
            px = proj(xs, mods[i], period=GRID_W, tiles_per_batch=lat_tiles, gated=True)
            pc = proj(cs, mods[i], period=CTX_LEN, tiles_per_batch=None, gated=not last)

            def stream(p, tokens):
                qt, k, vt, gates = p[:4]
                t_shape = (BATCH, tokens // SCAN_CHUNK, D_INNER, SCAN_CHUNK)
                return [k.reshape(BATCH, tokens, D_INNER), qt.reshape(t_shape), vt.reshape(t_shape)] + list(
                    _gate_prep(gates, tokens))

            hnx, hnc = _scan(stream(px, SEQ), stream(pc, CTX_LEN), ctx_out=not last)
            out = functools.partial(_mlstm_out, w_out=w_out, final_g=final_g2)
            if not last:
                cs = out(cs, mods[i], hnc.reshape(-1, D_INNER), *pc[4:], tiles_per_batch=None, final=False)
            xs = out(xs, mods[i], hnx.reshape(-1, D_INNER), *px[4:], tiles_per_batch=SEQ // OUT_ROWS, final=last)
    return xs.reshape(BATCH, SEQ, D_MODEL)
```

```python
import functools

import jax
import jax.numpy as jnp
from jax import lax
from jax.experimental import pallas as pl
from jax.experimental.pallas import tpu as pltpu

D_MODEL = 1024
BATCH = 16
SEQ = 2048
DEPTH = 4
GRID_W = 64
CTX_LEN = 256
D_INNER = 2 * D_MODEL
HEADS = 4
HEAD_DIM = D_INNER // HEADS
EPS = 1e-6

SCAN_CHUNK = 256
COL_BLOCK = 512
ROW_TILE = 1024
SUB_ROWS = 512
OUT_ROWS = 512
OUT_SUB_ROWS = 256
MOD_ROWS = 24
CTX_MOD_ROW = BATCH
GATE_LANES = 128
N_GATES = 4 * HEADS
VMEM_LIMIT_BYTES = 56 * 1024 * 1024

F32 = jnp.float32
BF16 = jnp.bfloat16


def _dot(a, b):
    return jnp.dot(a, b, preferred_element_type=F32)


def _sigmoid(x):
    return 0.5 * jnp.tanh(0.5 * x) + 0.5


def _silu(x):
    return x * _sigmoid(x)


def _params(*sem):
    return pltpu.CompilerParams(dimension_semantics=sem, vmem_limit_bytes=VMEM_LIMIT_BYTES)


def _mod_kernel(c_ref, w_ref, b_ref, o_ref):
    sc = _silu(c_ref[...])
    o_ref[...] = jnp.dot(sc, w_ref[...], preferred_element_type=F32,
                         precision=lax.Precision.HIGHEST) + b_ref[...]


def _modulation(c_all, mod_w, mod_b):
    nblk = 3 * D_MODEL // D_MODEL
    return pl.pallas_call(
        _mod_kernel,
        grid=(DEPTH, nblk),
        in_specs=[
            pl.BlockSpec((MOD_ROWS, D_MODEL), lambda l, j: (0, 0)),
            pl.BlockSpec((None, D_MODEL, D_MODEL), lambda l, j: (l, 0, j)),
            pl.BlockSpec((None, 1, D_MODEL), lambda l, j: (l, 0, j)),
        ],
        out_specs=pl.BlockSpec((None, MOD_ROWS, D_MODEL), lambda l, j: (l, 0, j)),
        out_shape=jax.ShapeDtypeStruct((DEPTH, MOD_ROWS, 3 * D_MODEL), F32),
        compiler_params=_params("arbitrary", "arbitrary"),
        name="modulation",
    )(c_all, mod_w, mod_b.reshape(DEPTH, 1, 3 * D_MODEL))


def _modulated_norm(x, g, mod):
    y = x * lax.rsqrt(jnp.mean(x * x, axis=-1, keepdims=True) + EPS) * g
    return y * (1.0 + mod[:, D_MODEL:2 * D_MODEL]) + mod[:, :D_MODEL]


def _conv3(u, w, period):
    rows = u.shape[0]
    t = lax.broadcasted_iota(jnp.int32, (rows, 1), 0) & (period - 1)
    prev = jnp.where(t != 0, pltpu.roll(u, 1, axis=0), 0.0)
    nxt = jnp.where(t != period - 1, pltpu.roll(u, rows - 1, axis=0), 0.0)
    return prev * w[0:1, :] + u * w[1:2, :] + nxt * w[2:3, :]


def _sub_tiles():
    return [pl.ds(r * SUB_ROWS, SUB_ROWS) for r in range(ROW_TILE // SUB_ROWS)]


def _mod_index(tiles_per_batch):
    if tiles_per_batch is None:
        return lambda i, *_: (CTX_MOD_ROW, 0, 0)
    return lambda i, *_: (i // tiles_per_batch, 0, 0)


def _conv_layer_kernel(x_ref, mod_ref, g_ref, win_ref, cw_ref, wo_ref, o_ref, *, period):
    for rows in _sub_tiles():
        hx = _modulated_norm(x_ref[rows, :], g_ref[...], mod_ref[...]).astype(BF16)
        acc = None
        for c in range(D_INNER // COL_BLOCK):
            col = lambda g, c=c: win_ref[:, pl.ds(g * D_INNER + c * COL_BLOCK, COL_BLOCK)]
            inner = pl.ds(c * COL_BLOCK, COL_BLOCK)
            cu = _dot(hx, col(1)) * _dot(hx, col(2))
            y = _dot(hx, col(0)) * _conv3(cu, cw_ref[:, inner], period)
            y = y * _silu(_dot(hx, col(3)))
            part = _dot(y.astype(BF16), wo_ref[inner, :])
            acc = part if acc is None else acc + part
        o_ref[rows, :] = x_ref[rows, :] + mod_ref[:, 2 * D_MODEL:] * acc


def _conv_layer(xs, mod, norm_g, w_in, conv_w, w_out, *, period, tiles_per_batch):
    rows = xs.shape[0]
    row_spec = pl.BlockSpec((ROW_TILE, D_MODEL), lambda i: (i, 0))
    resident = lambda shape: pl.BlockSpec(shape, lambda i: (0, 0), pipeline_mode=pl.Buffered(1))
    return pl.pallas_call(
        functools.partial(_conv_layer_kernel, period=period),
        grid=(rows // ROW_TILE,),
        in_specs=[
            row_spec,
            pl.BlockSpec((None, 1, 3 * D_MODEL), _mod_index(tiles_per_batch)),
            pl.BlockSpec((1, D_MODEL), lambda i: (0, 0)),
            resident((D_MODEL, 4 * D_INNER)),
            pl.BlockSpec((3, D_INNER), lambda i: (0, 0)),
            resident((D_INNER, D_MODEL)),
        ],
        out_specs=row_spec,
        out_shape=jax.ShapeDtypeStruct((rows, D_MODEL), F32),
        compiler_params=_params("parallel"),
        name="conv_layer",
    )(xs, mod, norm_g, w_in, conv_w, w_out)


def _mlstm_proj_kernel(x_ref, mod_ref, g_ref, wu_ref, wz_ref, wo_ref, cw_ref, wq_ref, wk_ref, wv_ref,
                       wg_ref, bg_ref, ng_ref, sk_ref, qt_ref, k_ref, vt_ref, gates_ref, *rest, period, gated):
    hx_ref = rest[-1]
    h = pl.program_id(1)

    @pl.when(h == 0)
    def _():
        for rows in _sub_tiles():
            hx_ref[rows, :] = _modulated_norm(x_ref[rows, :], g_ref[...], mod_ref[...]).astype(BF16)
        gates_ref[...] = jnp.broadcast_to(bg_ref[...], gates_ref.shape)

    subs = _sub_tiles()
    us = [_dot(hx_ref[rows, :], wu_ref[...]) for rows in subs]
    if gated:
        zs = [_dot(hx_ref[rows, :], wz_ref[...]) for rows in subs]
        os_ = [_dot(hx_ref[rows, :], wo_ref[...]) for rows in subs]
    ucs = [_silu(_conv3(u, cw_ref[...], period)) for u in us]

    qkv = []
    for u, uc in zip(us, ucs):
        ucb = uc.astype(BF16)
        qkv.append((_dot(ucb, wq_ref[...]), _dot(ucb, wk_ref[...]) * (HEAD_DIM ** -0.5),
                    _dot(u.astype(BF16), wv_ref[...])))

    for idx, rows in enumerate(subs):
        uc = ucs[idx]
        q, k, v = qkv[idx]
        k_ref[rows, :] = k.astype(BF16)
        for piece in range(SUB_ROWS // SCAN_CHUNK):
            chunk = rows.start // SCAN_CHUNK + piece
            part = slice(piece * SCAN_CHUNK, (piece + 1) * SCAN_CHUNK)
            qt = q[part, :].T.astype(BF16)
            kt = k[part, :].T.astype(BF16)
            vt = v[part, :].T.astype(BF16)
            qt_ref[chunk] = qt
            vt_ref[chunk] = vt
            cols = pl.ds(chunk * SCAN_CHUNK, SCAN_CHUNK)
            gates_ref[:, cols] += _dot(wg_ref[0], qt) + _dot(wg_ref[1], kt) + _dot(wg_ref[2], vt)
        if gated:
            a_ref, b_ref = rest[:2]
            sz = _silu(zs[idx])
            a_ref[rows, :] = (_sigmoid(os_[idx]) * ng_ref[...] * sz).astype(BF16)
            b_ref[rows, :] = (sk_ref[...] * uc * sz).astype(BF16)


def _mlstm_proj(xs, mod, norm_g, w_in, conv_w, wq, wk, wv, wg, bg, m_norm_g, m_skip, *, period,
                tiles_per_batch, gated):
    rows = xs.shape[0]
    w_in_spec = lambda g: pl.BlockSpec((D_MODEL, HEAD_DIM), lambda i, h: (0, g * HEADS + h))
    head_w = pl.BlockSpec((None, HEAD_DIM, HEAD_DIM), lambda i, h: (h, 0, 0))
    head_out = pl.BlockSpec((ROW_TILE, HEAD_DIM), lambda i, h: (i, h))
    head_vec = pl.BlockSpec((1, HEAD_DIM), lambda i, h: (0, h))
    inner = jax.ShapeDtypeStruct((rows, D_INNER), BF16)
    head_t = pl.BlockSpec((ROW_TILE // SCAN_CHUNK, HEAD_DIM, SCAN_CHUNK), lambda i, h: (i, h, 0))
    inner_t = jax.ShapeDtypeStruct((rows // SCAN_CHUNK, D_INNER, SCAN_CHUNK), BF16)
    n_inner = 5 if gated else 3
    outs = pl.pallas_call(
        functools.partial(_mlstm_proj_kernel, period=period, gated=gated),
        grid=(rows // ROW_TILE, HEADS),
        in_specs=[
            pl.BlockSpec((ROW_TILE, D_MODEL), lambda i, h: (i, 0)),
            pl.BlockSpec((None, 1, 3 * D_MODEL), _mod_index(tiles_per_batch)),
            pl.BlockSpec((1, D_MODEL), lambda i, h: (0, 0)),
            w_in_spec(0), w_in_spec(1), w_in_spec(2),
            pl.BlockSpec((3, HEAD_DIM), lambda i, h: (0, h)),
            head_w, head_w, head_w,
            pl.BlockSpec((3, None, N_GATES, HEAD_DIM), lambda i, h: (0, h, 0, 0)),
            pl.BlockSpec((N_GATES, 1), lambda i, h: (0, 0)),
            head_vec, head_vec,
        ],
        out_specs=[head_t, head_out, head_t, pl.BlockSpec((N_GATES, ROW_TILE), lambda i, h: (0, i))]
        + [head_out] * (n_inner - 3),
        out_shape=[inner_t, inner, inner_t, jax.ShapeDtypeStruct((N_GATES, rows), F32)]
        + [inner] * (n_inner - 3),
        scratch_shapes=[pltpu.VMEM((ROW_TILE, D_MODEL), BF16)],
        compiler_params=_params("parallel", "arbitrary"),
        name="mlstm_proj",
    )(xs, mod, norm_g, w_in, w_in, w_in, conv_w, wq, wk, wv, wg, bg, m_norm_g, m_skip)
    return outs


def _log_sigmoid(x):
    return jnp.minimum(x, 0.0) - jnp.log1p(jnp.exp(-jnp.abs(x)))


def _split3(x):
    hi = x.astype(BF16)
    r = x - hi.astype(F32)
    mid = r.astype(BF16)
    lo = (r - mid.astype(F32)).astype(BF16)
    return hi, mid, lo


def _running_max_lanes(x, reverse):
    n = x.shape[-1]
    lane = lax.broadcasted_iota(jnp.int32, x.shape, 1)
    shift = 1
    while shift < n:
        if reverse:
            moved = jnp.where(lane < n - shift, pltpu.roll(x, n - shift, axis=1), -jnp.inf)
        else:
            moved = jnp.where(lane >= shift, pltpu.roll(x, shift, axis=1), -jnp.inf)
        x = jnp.maximum(x, moved)
        shift *= 2
    return x


ROW_CR, ROW_CM, ROW_BCUM, ROW_BLAST = 0, 1, 2, 3
ROW_KINDS = 4


def _gate_prep_kernel(g_ref, crc_ref, row_ref, *, nchunks):
    L = SCAN_CHUNK
    G = 2 * HEADS
    r = lax.broadcasted_iota(jnp.int32, (L, L), 0)
    c = lax.broadcasted_iota(jnp.int32, (L, L), 1)
    lower = (c <= r).astype(BF16)
    upper = (c >= r).astype(BF16)
    fwd_row = lax.broadcasted_iota(jnp.int32, (G, 1), 0) < HEADS
    for ci in range(nchunks):
        rows = pl.ds(ci * L, L)
        g = g_ref[:, rows]
        parts = _split3(_log_sigmoid(g))
        pre = sum(_dot(p, upper) for p in parts)[G:, :]
        suf = sum(_dot(p, lower) for p in parts)[G:, :]
        bcum = jnp.where(fwd_row, pre, suf)
        cr = g[0:G, :] - bcum
        b_last = jnp.where(fwd_row, pre[:, L - 1:L], suf[:, 0:1])
        row_ref[ci, ROW_CR * G:(ROW_CR + 1) * G, :] = cr
        row_ref[ci, ROW_CM * G:(ROW_CM + 1) * G, :] = jnp.where(
            fwd_row, _running_max_lanes(cr, False), _running_max_lanes(cr, True))
        row_ref[ci, ROW_BCUM * G:(ROW_BCUM + 1) * G, :] = bcum
        row_ref[ci, ROW_BLAST * G:(ROW_BLAST + 1) * G, :] = jnp.broadcast_to(b_last, (G, L))
        crc_ref[rows, :] = jnp.concatenate([cr, jnp.zeros((GATE_LANES - G, L), F32)], axis=0).T


def _gate_prep(gates, tokens):
    nchunks = tokens // SCAN_CHUNK
    row_rows = ROW_KINDS * 2 * HEADS
    return pl.pallas_call(
        functools.partial(_gate_prep_kernel, nchunks=nchunks),
        grid=(BATCH,),
        in_specs=[pl.BlockSpec((N_GATES, tokens), lambda b: (0, b))],
        out_specs=[
            pl.BlockSpec((None, tokens, GATE_LANES), lambda b: (b, 0, 0)),
            pl.BlockSpec((None, nchunks, row_rows, SCAN_CHUNK), lambda b: (b, 0, 0, 0)),
        ],
        out_shape=[
            jax.ShapeDtypeStruct((BATCH, tokens, GATE_LANES), F32),
            jax.ShapeDtypeStruct((BATCH, nchunks, row_rows, SCAN_CHUNK), F32),
        ],
        compiler_params=_params("parallel"),
        name="gate_prep",
    )(gates)


class _Seq:
    def __init__(self, k, qt, vt, crc, row, hs, hn, nchunks):
        self.k, self.qt, self.vt, self.crc, self.row = k, qt, vt, crc, row
        self.hs, self.hn, self.nchunks = hs, hn, nchunks


def _chunk_rows(ci):
    start = ci * SCAN_CHUNK
    return pl.ds(start if isinstance(ci, int) else pl.multiple_of(start, SCAN_CHUNK), SCAN_CHUNK)


def _dot_row_halves(a, b, a2=None, b2=None):
    rows = a.shape[0]
    cut = (rows // 32) * 16
    parts = []
    for part in (slice(0, cut), slice(cut, rows)):
        acc = _dot(a[part], b)
        if a2 is not None:
            acc = acc + _dot(a2[part], b2)
        parts.append(acc)
    return jnp.concatenate(parts, axis=0)


def _direction_step(seq, ci, direction, head, ct_ref, n, m):
    L = SCAN_CHUNK
    rows = _chunk_rows(ci)
    gate = direction * HEADS + head
    row = lambda kind: seq.row[ci, pl.ds(kind * 2 * HEADS + gate, 1), :]
    k = seq.k[rows, :]
    qt = seq.qt[ci]
    vt = seq.vt[ci]
    cr_row, cm_row = row(ROW_CR), row(ROW_CM)
    top_j = jnp.maximum(m, cm_row)
    want_h = seq.hn is not None
    ct = ct_ref[...]

    if want_h:
        n_rows = jnp.broadcast_to(n.astype(BF16), (16, HEAD_DIM))
        kq = _dot_row_halves(jnp.concatenate([k, n_rows, ct.astype(BF16)], axis=0), qt)

    top = jnp.maximum(m, jnp.max(cr_row, axis=-1, keepdims=True))
    w_decay = jnp.exp(m - top)
    ws = jnp.exp(cr_row - top).astype(BF16)
    upd = _dot(jnp.concatenate([vt * ws, jnp.broadcast_to(ws, (16, L))], axis=0), k)

    h = None
    if want_h:
        lane_sel = lax.broadcasted_iota(jnp.int32, (1, GATE_LANES), 1) == gate
        cr_col = jnp.sum(jnp.where(lane_sel, seq.crc[rows, :], 0.0), axis=-1, keepdims=True)
        s = lax.broadcasted_iota(jnp.int32, (L, L), 0)
        j = lax.broadcasted_iota(jnp.int32, (L, L), 1)
        seen = (s <= j) if direction == 0 else (s >= j)
        decay = jnp.exp(jnp.where(seen, cr_col - cm_row, -jnp.inf))
        sc = kq[:L] * decay
        w_intra = jnp.exp(cm_row - top_j)
        w_inter = jnp.exp(m - top_j)
        den = w_inter * kq[L:L + 1] + w_intra * jnp.sum(sc, axis=0, keepdims=True)
        inv = 1.0 / jnp.maximum(jnp.abs(den), jnp.exp(-(row(ROW_BCUM) + top_j)))
        ss = (sc * (w_intra * inv)).astype(BF16)
        h = (w_inter * inv) * kq[L + 16:] + _dot_row_halves(vt, ss)

    ct_ref[...] = w_decay * ct + upd[:HEAD_DIM]
    n_new = w_decay * n + upd[HEAD_DIM:HEAD_DIM + 1]
    m_new = row(ROW_BLAST)[:, 0:1] + top
    return h, n_new, m_new


def _head_norm_t(hsum):
    hc = hsum - jnp.mean(hsum, axis=0, keepdims=True)
    var = jnp.mean(hc * hc, axis=0, keepdims=True)
    return (hc * lax.rsqrt(var + EPS)).T.astype(BF16)


def _scan_stream(seq, direction, head, ct_ref, state):
    nch = seq.nchunks

    def body(i, st):
        ci = i if direction == 0 else nch - 1 - i
        h, n, m = _direction_step(seq, ci, direction, head, ct_ref, *st)
        if seq.hn is not None:
            if direction == 0:
                seq.hs[ci] = h
            else:
                seq.hn[_chunk_rows(ci), :] = _head_norm_t(seq.hs[ci] + h)
        return n, m

    for i in range(nch):
        state = body(i, state)
    return state


def _scan_kernel(*refs, ctx_out):
    kx, qtx, vtx, crcx, rowx, kc, qtc, vtc, crcc, rowc = refs[:10]
    refs = list(refs[10:])
    hnx = refs.pop(0)
    hnc = refs.pop(0) if ctx_out else None
    ct_ref, hsx = refs[:2]
    hsc = refs[2] if ctx_out else None
    head = pl.program_id(1)
    ctx = _Seq(kc, qtc, vtc, crcc, rowc, hsc, hnc, CTX_LEN // SCAN_CHUNK)
    lat = _Seq(kx, qtx, vtx, crcx, rowx, hsx, hnx, SEQ // SCAN_CHUNK)
    for direction in (0, 1):
        ct_ref[...] = jnp.zeros_like(ct_ref)
        state = (jnp.zeros((1, HEAD_DIM), F32), jnp.zeros((1, 1), F32))
        state = _scan_stream(ctx, direction, head, ct_ref, state)
        _scan_stream(lat, direction, head, ct_ref, state)


def _scan(lat, ctx, *, ctx_out):
    def specs(tokens):
        nch = tokens // SCAN_CHUNK
        qv_t = pl.BlockSpec((None, nch, HEAD_DIM, SCAN_CHUNK), lambda b, h: (b, 0, h, 0))
        return [pl.BlockSpec((None, tokens, HEAD_DIM), lambda b, h: (b, 0, h)), qv_t, qv_t,
                pl.BlockSpec((None, tokens, GATE_LANES), lambda b, h: (b, 0, 0)),
                pl.BlockSpec((None, nch, ROW_KINDS * 2 * HEADS, SCAN_CHUNK), lambda b, h: (b, 0, 0, 0))]

    hn_spec = lambda tokens: pl.BlockSpec((None, tokens, HEAD_DIM), lambda b, h: (b, 0, h))
    hn_shape = lambda tokens: jax.ShapeDtypeStruct((BATCH, tokens, D_INNER), BF16)
    hs_shape = lambda tokens: pltpu.VMEM((tokens // SCAN_CHUNK, HEAD_DIM, SCAN_CHUNK), F32)
    out_specs = [hn_spec(SEQ)] + ([hn_spec(CTX_LEN)] if ctx_out else [])
    out_shape = [hn_shape(SEQ)] + ([hn_shape(CTX_LEN)] if ctx_out else [])
    scratch = [pltpu.VMEM((HEAD_DIM, HEAD_DIM), F32), hs_shape(SEQ)] + ([hs_shape(CTX_LEN)] if ctx_out else [])
    outs = pl.pallas_call(
        functools.partial(_scan_kernel, ctx_out=ctx_out),
        grid=(BATCH, HEADS),
        in_specs=specs(SEQ) + specs(CTX_LEN),
        out_specs=out_specs,
        out_shape=out_shape,
        scratch_shapes=scratch,
        compiler_params=_params("parallel", "arbitrary"),
        name="mlstm_scan",
    )(*lat, *ctx)
    return outs if ctx_out else (outs[0], None)


def _mlstm_out_kernel(x_ref, mod_ref, hn_ref, a_ref, b_ref, wo_ref, fg_ref, out_ref, *, final):
    for r in range(OUT_ROWS // OUT_SUB_ROWS):
        rows = pl.ds(r * OUT_SUB_ROWS, OUT_SUB_ROWS)
        y = hn_ref[rows, :].astype(F32) * a_ref[rows, :].astype(F32) + b_ref[rows, :].astype(F32)
        xn = x_ref[rows, :] + mod_ref[:, 2 * D_MODEL:] * _dot(y.astype(BF16), wo_ref[...])
        if final:
            xn = xn * lax.rsqrt(jnp.mean(xn * xn, axis=-1, keepdims=True) + EPS) * fg_ref[...]
        out_ref[rows, :] = xn


def _mlstm_out(xs, mod, hn, a, b, w_out, final_g, *, tiles_per_batch, final):
    rows = xs.shape[0]
    row_spec = pl.BlockSpec((OUT_ROWS, D_MODEL), lambda i: (i, 0))
    inner = pl.BlockSpec((OUT_ROWS, D_INNER), lambda i: (i, 0))
    return pl.pallas_call(
        functools.partial(_mlstm_out_kernel, final=final),
        grid=(rows // OUT_ROWS,),
        in_specs=[
            row_spec,
            pl.BlockSpec((None, 1, 3 * D_MODEL), _mod_index(tiles_per_batch)),
            inner, inner, inner,
            pl.BlockSpec((D_INNER, D_MODEL), lambda i: (0, 0)),
            pl.BlockSpec((1, D_MODEL), lambda i: (0, 0)),
        ],
        out_specs=row_spec,
        out_shape=jax.ShapeDtypeStruct((rows, D_MODEL), F32),
        compiler_params=_params("parallel"),
        name="mlstm_out",
    )(xs, mod, hn, a, b, w_out, final_g)


def _gate_weights(w_gate, b_gate):
    def order(a):
        a = a.reshape(a.shape[:-1] + (4, HEADS))
        return jnp.concatenate([a[..., 0::2, :], a[..., 1::2, :]], axis=-2).reshape(a.shape[:-2] + (N_GATES,))

    wg = jnp.swapaxes(order(w_gate).reshape(3, HEADS, HEAD_DIM, N_GATES), -1, -2).astype(BF16)
    return wg, order(b_gate).reshape(N_GATES, 1)


def kernel(x, c, ctx, c_ctx, norm_g, mod_w, mod_b, conv_w_in, conv_w, conv_w_out, m_w_in, m_conv_w, m_wq,
           m_wk, m_wv, m_w_gate, m_b_gate, m_norm_g, m_skip, m_w_out, final_g):
    assert x.shape == (BATCH, SEQ, D_MODEL) and ctx.shape == (BATCH, CTX_LEN, D_MODEL)
    c_all = jnp.zeros((MOD_ROWS, D_MODEL), F32).at[:BATCH].set(c).at[CTX_MOD_ROW].set(c_ctx)
    mods = _modulation(c_all, mod_w, mod_b).reshape(DEPTH, MOD_ROWS, 1, 3 * D_MODEL)
    xs = x.reshape(BATCH * SEQ, D_MODEL)
    cs = ctx.reshape(BATCH * CTX_LEN, D_MODEL)
    lat_tiles = SEQ // ROW_TILE
    final_g2 = final_g.reshape(1, D_MODEL)
    for i in range(DEPTH):
        last = i == DEPTH - 1
        j = i // 2
        g = norm_g[i].reshape(1, D_MODEL)
        if i % 2 == 0:
            w_in, w_out = conv_w_in[j].astype(BF16), conv_w_out[j].astype(BF16)
            xs = _conv_layer(xs, mods[i], g, w_in, conv_w[j], w_out, period=GRID_W, tiles_per_batch=lat_tiles)
            if not last:
                cs = _conv_layer(cs, mods[i], g, w_in, conv_w[j], w_out, period=CTX_LEN, tiles_per_batch=None)
        else:
            w_in, w_out = m_w_in[j].astype(BF16), m_w_out[j].astype(BF16)
            wq, wk, wv = m_wq[j].astype(BF16), m_wk[j].astype(BF16), m_wv[j].astype(BF16)
            wg, bg = _gate_weights(m_w_gate[j], m_b_gate[j])
            proj = functools.partial(_mlstm_proj, norm_g=g, w_in=w_in, conv_w=m_conv_w[j], wq=wq, wk=wk, wv=wv,
                                     wg=wg, bg=bg, m_norm_g=m_norm_g[j].reshape(1, D_INNER),
                                     m_skip=m_skip[j].reshape(1, D_INNER))
            px = proj(xs, mods[i], period=GRID_W, tiles_per_batch=lat_tiles, gated=True)
            pc = proj(cs, mods[i], period=CTX_LEN, tiles_per_batch=None, gated=not last)

            def stream(p, tokens):
                qt, k, vt, gates = p[:4]
                t_shape = (BATCH, tokens // SCAN_CHUNK, D_INNER, SCAN_CHUNK)
                return [k.reshape(BATCH, tokens, D_INNER), qt.reshape(t_shape), vt.reshape(t_shape)] + list(
                    _gate_prep(gates, tokens))

            hnx, hnc = _scan(stream(px, SEQ), stream(pc, CTX_LEN), ctx_out=not last)
            out = functools.partial(_mlstm_out, w_out=w_out, final_g=final_g2)
            if not last:
                cs = out(cs, mods[i], hnc.reshape(-1, D_INNER), *pc[4:], tiles_per_batch=None, final=False)
            xs = out(xs, mods[i], hnx.reshape(-1, D_INNER), *px[4:], tiles_per_batch=SEQ // OUT_ROWS, final=last)
    return xs.reshape(BATCH, SEQ, D_MODEL)
```

```python
import functools

import jax
import jax.numpy as jnp
from jax import lax
from jax.experimental import pallas as pl
from jax.experimental.pallas import tpu as pltpu

D_MODEL = 1024
BATCH = 16
SEQ = 2048
DEPTH = 4
GRID_W = 64
CTX_LEN = 256
D_INNER = 2 * D_MODEL
HEADS = 4
HEAD_DIM = D_INNER // HEADS
EPS = 1e-6

SCAN_CHUNK = 256
COL_BLOCK = 1024
ROW_TILE = 1024
SUB_ROWS = 512
OUT_ROWS = 512
OUT_SUB_ROWS = 256
MOD_ROWS = 24
CTX_MOD_ROW = BATCH
GATE_LANES = 128
N_GATES = 4 * HEADS
VMEM_LIMIT_BYTES = 56 * 1024 * 1024

F32 = jnp.float32
BF16 = jnp.bfloat16


def _dot(a, b):
    return jnp.dot(a, b, preferred_element_type=F32)


def _sigmoid(x):
    return 0.5 * jnp.tanh(0.5 * x) + 0.5


def _silu(x):
    return x * _sigmoid(x)


def _params(*sem):
    return pltpu.CompilerParams(dimension_semantics=sem, vmem_limit_bytes=VMEM_LIMIT_BYTES)


def _mod_kernel(c_ref, w_ref, b_ref, o_ref):
    sc = _silu(c_ref[...])
    o_ref[...] = jnp.dot(sc, w_ref[...], preferred_element_type=F32,
                         precision=lax.Precision.HIGHEST) + b_ref[...]


def _modulation(c_all, mod_w, mod_b):
    nblk = 3 * D_MODEL // D_MODEL
    return pl.pallas_call(
        _mod_kernel,
        grid=(DEPTH, nblk),
        in_specs=[
            pl.BlockSpec((MOD_ROWS, D_MODEL), lambda l, j: (0, 0)),
            pl.BlockSpec((None, D_MODEL, D_MODEL), lambda l, j: (l, 0, j)),
            pl.BlockSpec((None, 1, D_MODEL), lambda l, j: (l, 0, j)),
        ],
        out_specs=pl.BlockSpec((None, MOD_ROWS, D_MODEL), lambda l, j: (l, 0, j)),
        out_shape=jax.ShapeDtypeStruct((DEPTH, MOD_ROWS, 3 * D_MODEL), F32),
        compiler_params=_params("arbitrary", "arbitrary"),
        name="modulation",
    )(c_all, mod_w, mod_b.reshape(DEPTH, 1, 3 * D_MODEL))


def _modulated_norm(x, g, mod):
    y = x * lax.rsqrt(jnp.mean(x * x, axis=-1, keepdims=True) + EPS) * g
    return y * (1.0 + mod[:, D_MODEL:2 * D_MODEL]) + mod[:, :D_MODEL]


def _conv3(u, w, period):
    rows = u.shape[0]
    t = lax.broadcasted_iota(jnp.int32, (rows, 1), 0) & (period - 1)
    prev = jnp.where(t != 0, pltpu.roll(u, 1, axis=0), 0.0)
    nxt = jnp.where(t != period - 1, pltpu.roll(u, rows - 1, axis=0), 0.0)
    return prev * w[0:1, :] + u * w[1:2, :] + nxt * w[2:3, :]


def _sub_tiles():
    return [pl.ds(r * SUB_ROWS, SUB_ROWS) for r in range(ROW_TILE // SUB_ROWS)]


def _mod_index(tiles_per_batch):
    if tiles_per_batch is None:
        return lambda i, *_: (CTX_MOD_ROW, 0, 0)
    return lambda i, *_: (i // tiles_per_batch, 0, 0)


def _conv_layer_kernel(x_ref, mod_ref, g_ref, win_ref, cw_ref, wo_ref, o_ref, *, period):
    for rows in _sub_tiles():
        hx = _modulated_norm(x_ref[rows, :], g_ref[...], mod_ref[...]).astype(BF16)
        acc = None
        for c in range(D_INNER // COL_BLOCK):
            col = lambda g, c=c: win_ref[:, pl.ds(g * D_INNER + c * COL_BLOCK, COL_BLOCK)]
            inner = pl.ds(c * COL_BLOCK, COL_BLOCK)
            cu = _dot(hx, col(1)) * _dot(hx, col(2))
            y = _dot(hx, col(0)) * _conv3(cu, cw_ref[:, inner], period)
            y = y * _silu(_dot(hx, col(3)))
            part = _dot(y.astype(BF16), wo_ref[inner, :])
            acc = part if acc is None else acc + part
        o_ref[rows, :] = x_ref[rows, :] + mod_ref[:, 2 * D_MODEL:] * acc


def _conv_layer(xs, mod, norm_g, w_in, conv_w, w_out, *, period, tiles_per_batch):
    rows = xs.shape[0]
    row_spec = pl.BlockSpec((ROW_TILE, D_MODEL), lambda i: (i, 0))
    resident = lambda shape: pl.BlockSpec(shape, lambda i: (0, 0), pipeline_mode=pl.Buffered(1))
    return pl.pallas_call(
        functools.partial(_conv_layer_kernel, period=period),
        grid=(rows // ROW_TILE,),
        in_specs=[
            row_spec,
            pl.BlockSpec((None, 1, 3 * D_MODEL), _mod_index(tiles_per_batch)),
            pl.BlockSpec((1, D_MODEL), lambda i: (0, 0)),
            resident((D_MODEL, 4 * D_INNER)),
            pl.BlockSpec((3, D_INNER), lambda i: (0, 0)),
            resident((D_INNER, D_MODEL)),
        ],
        out_specs=row_spec,
        out_shape=jax.ShapeDtypeStruct((rows, D_MODEL), F32),
        compiler_params=_params("parallel"),
        name="conv_layer",
    )(xs, mod, norm_g, w_in, conv_w, w_out)


def _mlstm_proj_kernel(x_ref, mod_ref, g_ref, wu_ref, wz_ref, wo_ref, cw_ref, wq_ref, wk_ref, wv_ref,
                       wg_ref, bg_ref, ng_ref, sk_ref, qt_ref, k_ref, vt_ref, gates_ref, *rest, period, gated):
    hx_ref = rest[-1]
    h = pl.program_id(1)

    @pl.when(h == 0)
    def _():
        for rows in _sub_tiles():
            hx_ref[rows, :] = _modulated_norm(x_ref[rows, :], g_ref[...], mod_ref[...]).astype(BF16)
        gates_ref[...] = jnp.broadcast_to(bg_ref[...], gates_ref.shape)

    subs = _sub_tiles()
    us = [_dot(hx_ref[rows, :], wu_ref[...]) for rows in subs]
    if gated:
        zs = [_dot(hx_ref[rows, :], wz_ref[...]) for rows in subs]
        os_ = [_dot(hx_ref[rows, :], wo_ref[...]) for rows in subs]
    ucs = [_silu(_conv3(u, cw_ref[...], period)) for u in us]

    qkv = []
    for u, uc in zip(us, ucs):
        ucb = uc.astype(BF16)
        qkv.append((_dot(ucb, wq_ref[...]), _dot(ucb, wk_ref[...]) * (HEAD_DIM ** -0.5),
                    _dot(u.astype(BF16), wv_ref[...])))

    for idx, rows in enumerate(subs):
        uc = ucs[idx]
        q, k, v = qkv[idx]
        k_ref[rows, :] = k.astype(BF16)
        for piece in range(SUB_ROWS // SCAN_CHUNK):
            chunk = rows.start // SCAN_CHUNK + piece
            part = slice(piece * SCAN_CHUNK, (piece + 1) * SCAN_CHUNK)
            qt = q[part, :].T.astype(BF16)
            kt = k[part, :].T.astype(BF16)
            vt = v[part, :].T.astype(BF16)
            qt_ref[chunk] = qt
            vt_ref[chunk] = vt
            cols = pl.ds(chunk * SCAN_CHUNK, SCAN_CHUNK)
            gates_ref[:, cols] += _dot(wg_ref[0], qt) + _dot(wg_ref[1], kt) + _dot(wg_ref[2], vt)
        if gated:
            a_ref, b_ref = rest[:2]
            sz = _silu(zs[idx])
            a_ref[rows, :] = (_sigmoid(os_[idx]) * ng_ref[...] * sz).astype(BF16)
            b_ref[rows, :] = (sk_ref[...] * uc * sz).astype(BF16)


def _mlstm_proj(xs, mod, norm_g, w_in, conv_w, wq, wk, wv, wg, bg, m_norm_g, m_skip, *, period,
                tiles_per_batch, gated):
    rows = xs.shape[0]
    w_in_spec = lambda g: pl.BlockSpec((D_MODEL, HEAD_DIM), lambda i, h: (0, g * HEADS + h))
    head_w = pl.BlockSpec((None, HEAD_DIM, HEAD_DIM), lambda i, h: (h, 0, 0))
    head_out = pl.BlockSpec((ROW_TILE, HEAD_DIM), lambda i, h: (i, h))
    head_vec = pl.BlockSpec((1, HEAD_DIM), lambda i, h: (0, h))
    inner = jax.ShapeDtypeStruct((rows, D_INNER), BF16)
    head_t = pl.BlockSpec((ROW_TILE // SCAN_CHUNK, HEAD_DIM, SCAN_CHUNK), lambda i, h: (i, h, 0))
    inner_t = jax.ShapeDtypeStruct((rows // SCAN_CHUNK, D_INNER, SCAN_CHUNK), BF16)
    n_inner = 5 if gated else 3
    outs = pl.pallas_call(
        functools.partial(_mlstm_proj_kernel, period=period, gated=gated),
        grid=(rows // ROW_TILE, HEADS),
        in_specs=[
            pl.BlockSpec((ROW_TILE, D_MODEL), lambda i, h: (i, 0)),
            pl.BlockSpec((None, 1, 3 * D_MODEL), _mod_index(tiles_per_batch)),
            pl.BlockSpec((1, D_MODEL), lambda i, h: (0, 0)),
            w_in_spec(0), w_in_spec(1), w_in_spec(2),
            pl.BlockSpec((3, HEAD_DIM), lambda i, h: (0, h)),
            head_w, head_w, head_w,
            pl.BlockSpec((3, None, N_GATES, HEAD_DIM), lambda i, h: (0, h, 0, 0)),
            pl.BlockSpec((N_GATES, 1), lambda i, h: (0, 0)),
            head_vec, head_vec,
        ],
        out_specs=[head_t, head_out, head_t, pl.BlockSpec((N_GATES, ROW_TILE), lambda i, h: (0, i))]
        + [head_out] * (n_inner - 3),
        out_shape=[inner_t, inner, inner_t, jax.ShapeDtypeStruct((N_GATES, rows), F32)]
        + [inner] * (n_inner - 3),
        scratch_shapes=[pltpu.VMEM((ROW_TILE, D_MODEL), BF16)],
        compiler_params=_params("parallel", "arbitrary"),
        name="mlstm_proj",
    )(xs, mod, norm_g, w_in, w_in, w_in, conv_w, wq, wk, wv, wg, bg, m_norm_g, m_skip)
    return outs


def _log_sigmoid(x):
    return jnp.minimum(x, 0.0) - jnp.log1p(jnp.exp(-jnp.abs(x)))


def _split3(x):
    hi = x.astype(BF16)
    r = x - hi.astype(F32)
    mid = r.astype(BF16)
    lo = (r - mid.astype(F32)).astype(BF16)
    return hi, mid, lo


def _running_max_lanes(x, reverse):
    n = x.shape[-1]
    lane = lax.broadcasted_iota(jnp.int32, x.shape, 1)
    shift = 1
    while shift < n:
        if reverse:
            moved = jnp.where(lane < n - shift, pltpu.roll(x, n - shift, axis=1), -jnp.inf)
        else:
            moved = jnp.where(lane >= shift, pltpu.roll(x, shift, axis=1), -jnp.inf)
        x = jnp.maximum(x, moved)
        shift *= 2
    return x


ROW_CR, ROW_CM, ROW_BCUM, ROW_BLAST = 0, 1, 2, 3
ROW_KINDS = 4


def _gate_prep_kernel(g_ref, crc_ref, row_ref, *, nchunks):
    L = SCAN_CHUNK
    G = 2 * HEADS
    r = lax.broadcasted_iota(jnp.int32, (L, L), 0)
    c = lax.broadcasted_iota(jnp.int32, (L, L), 1)
    lower = (c <= r).astype(BF16)
    upper = (c >= r).astype(BF16)
    fwd_row = lax.broadcasted_iota(jnp.int32, (G, 1), 0) < HEADS
    for ci in range(nchunks):
        rows = pl.ds(ci * L, L)
        g = g_ref[:, rows]
        parts = _split3(_log_sigmoid(g))
        pre = sum(_dot(p, upper) for p in parts)[G:, :]
        suf = sum(_dot(p, lower) for p in parts)[G:, :]
        bcum = jnp.where(fwd_row, pre, suf)
        cr = g[0:G, :] - bcum
        b_last = jnp.where(fwd_row, pre[:, L - 1:L], suf[:, 0:1])
        row_ref[ci, ROW_CR * G:(ROW_CR + 1) * G, :] = cr
        row_ref[ci, ROW_CM * G:(ROW_CM + 1) * G, :] = jnp.where(
            fwd_row, _running_max_lanes(cr, False), _running_max_lanes(cr, True))
        row_ref[ci, ROW_BCUM * G:(ROW_BCUM + 1) * G, :] = bcum
        row_ref[ci, ROW_BLAST * G:(ROW_BLAST + 1) * G, :] = jnp.broadcast_to(b_last, (G, L))
        crc_ref[rows, :] = jnp.concatenate([cr, jnp.zeros((GATE_LANES - G, L), F32)], axis=0).T


def _gate_prep(gates, tokens):
    nchunks = tokens // SCAN_CHUNK
    row_rows = ROW_KINDS * 2 * HEADS
    return pl.pallas_call(
        functools.partial(_gate_prep_kernel, nchunks=nchunks),
        grid=(BATCH,),
        in_specs=[pl.BlockSpec((N_GATES, tokens), lambda b: (0, b))],
        out_specs=[
            pl.BlockSpec((None, tokens, GATE_LANES), lambda b: (b, 0, 0)),
            pl.BlockSpec((None, nchunks, row_rows, SCAN_CHUNK), lambda b: (b, 0, 0, 0)),
        ],
        out_shape=[
            jax.ShapeDtypeStruct((BATCH, tokens, GATE_LANES), F32),
            jax.ShapeDtypeStruct((BATCH, nchunks, row_rows, SCAN_CHUNK), F32),
        ],
        compiler_params=_params("parallel"),
        name="gate_prep",
    )(gates)


class _Seq:
    def __init__(self, k, qt, vt, crc, row, hs, hn, nchunks):
        self.k, self.qt, self.vt, self.crc, self.row = k, qt, vt, crc, row
        self.hs, self.hn, self.nchunks = hs, hn, nchunks


def _chunk_rows(ci):
    start = ci * SCAN_CHUNK
    return pl.ds(start if isinstance(ci, int) else pl.multiple_of(start, SCAN_CHUNK), SCAN_CHUNK)


def _dot_row_halves(a, b, a2=None, b2=None):
    rows = a.shape[0]
    cut = (rows // 32) * 16
    parts = []
    for part in (slice(0, cut), slice(cut, rows)):
        acc = _dot(a[part], b)
        if a2 is not None:
            acc = acc + _dot(a2[part], b2)
        parts.append(acc)
    return jnp.concatenate(parts, axis=0)


def _direction_step(seq, ci, direction, head, ct_ref, n, m):
    L = SCAN_CHUNK
    rows = _chunk_rows(ci)
    gate = direction * HEADS + head
    row = lambda kind: seq.row[ci, pl.ds(kind * 2 * HEADS + gate, 1), :]
    k = seq.k[rows, :]
    qt = seq.qt[ci]
    vt = seq.vt[ci]
    cr_row, cm_row = row(ROW_CR), row(ROW_CM)
    top_j = jnp.maximum(m, cm_row)
    want_h = seq.hn is not None
    ct = ct_ref[...]

    if want_h:
        n_rows = jnp.broadcast_to(n.astype(BF16), (16, HEAD_DIM))
        kq = _dot_row_halves(jnp.concatenate([k, n_rows, ct.astype(BF16)], axis=0), qt)

    top = jnp.maximum(m, jnp.max(cr_row, axis=-1, keepdims=True))
    w_decay = jnp.exp(m - top)
    ws = jnp.exp(cr_row - top).astype(BF16)
    upd = _dot(jnp.concatenate([vt * ws, jnp.broadcast_to(ws, (16, L))], axis=0), k)

    h = None
    if want_h:
        lane_sel = lax.broadcasted_iota(jnp.int32, (1, GATE_LANES), 1) == gate
        cr_col = jnp.sum(jnp.where(lane_sel, seq.crc[rows, :], 0.0), axis=-1, keepdims=True)
        s = lax.broadcasted_iota(jnp.int32, (L, L), 0)
        j = lax.broadcasted_iota(jnp.int32, (L, L), 1)
        seen = (s <= j) if direction == 0 else (s >= j)
        decay = jnp.exp(jnp.where(seen, cr_col - cm_row, -jnp.inf))
        sc = kq[:L] * decay
        w_intra = jnp.exp(cm_row - top_j)
        w_inter = jnp.exp(m - top_j)
        den = w_inter * kq[L:L + 1] + w_intra * jnp.sum(sc, axis=0, keepdims=True)
        inv = 1.0 / jnp.maximum(jnp.abs(den), jnp.exp(-(row(ROW_BCUM) + top_j)))
        ss = (sc * (w_intra * inv)).astype(BF16)
        h = (w_inter * inv) * kq[L + 16:] + _dot_row_halves(vt, ss)

    ct_ref[...] = w_decay * ct + upd[:HEAD_DIM]
    n_new = w_decay * n + upd[HEAD_DIM:HEAD_DIM + 1]
    m_new = row(ROW_BLAST)[:, 0:1] + top
    return h, n_new, m_new


def _head_norm_t(hsum):
    hc = hsum - jnp.mean(hsum, axis=0, keepdims=True)
    var = jnp.mean(hc * hc, axis=0, keepdims=True)
    return (hc * lax.rsqrt(var + EPS)).T.astype(BF16)


def _scan_stream(seq, direction, head, ct_ref, state):
    nch = seq.nchunks

    def body(i, st):
        ci = i if direction == 0 else nch - 1 - i
        h, n, m = _direction_step(seq, ci, direction, head, ct_ref, *st)
        if seq.hn is not None:
            if direction == 0:
                seq.hs[ci] = h
            else:
                seq.hn[_chunk_rows(ci), :] = _head_norm_t(seq.hs[ci] + h)
        return n, m

    for i in range(nch):
        state = body(i, state)
    return state


def _scan_kernel(*refs, ctx_out):
    kx, qtx, vtx, crcx, rowx, kc, qtc, vtc, crcc, rowc = refs[:10]
    refs = list(refs[10:])
    hnx = refs.pop(0)
    hnc = refs.pop(0) if ctx_out else None
    ct_ref, hsx = refs[:2]
    hsc = refs[2] if ctx_out else None
    head = pl.program_id(1)
    ctx = _Seq(kc, qtc, vtc, crcc, rowc, hsc, hnc, CTX_LEN // SCAN_CHUNK)
    lat = _Seq(kx, qtx, vtx, crcx, rowx, hsx, hnx, SEQ // SCAN_CHUNK)
    for direction in (0, 1):
        ct_ref[...] = jnp.zeros_like(ct_ref)
        state = (jnp.zeros((1, HEAD_DIM), F32), jnp.zeros((1, 1), F32))
        state = _scan_stream(ctx, direction, head, ct_ref, state)
        _scan_stream(lat, direction, head, ct_ref, state)


def _scan(lat, ctx, *, ctx_out):
    def specs(tokens):
        nch = tokens // SCAN_CHUNK
        qv_t = pl.BlockSpec((None, nch, HEAD_DIM, SCAN_CHUNK), lambda b, h: (b, 0, h, 0))
        return [pl.BlockSpec((None, tokens, HEAD_DIM), lambda b, h: (b, 0, h)), qv_t, qv_t,
                pl.BlockSpec((None, tokens, GATE_LANES), lambda b, h: (b, 0, 0)),
                pl.BlockSpec((None, nch, ROW_KINDS * 2 * HEADS, SCAN_CHUNK), lambda b, h: (b, 0, 0, 0))]

    hn_spec = lambda tokens: pl.BlockSpec((None, tokens, HEAD_DIM), lambda b, h: (b, 0, h))
    hn_shape = lambda tokens: jax.ShapeDtypeStruct((BATCH, tokens, D_INNER), BF16)
    hs_shape = lambda tokens: pltpu.VMEM((tokens // SCAN_CHUNK, HEAD_DIM, SCAN_CHUNK), F32)
    out_specs = [hn_spec(SEQ)] + ([hn_spec(CTX_LEN)] if ctx_out else [])
    out_shape = [hn_shape(SEQ)] + ([hn_shape(CTX_LEN)] if ctx_out else [])
    scratch = [pltpu.VMEM((HEAD_DIM, HEAD_DIM), F32), hs_shape(SEQ)] + ([hs_shape(CTX_LEN)] if ctx_out else [])
    outs = pl.pallas_call(
        functools.partial(_scan_kernel, ctx_out=ctx_out),
        grid=(BATCH, HEADS),
        in_specs=specs(SEQ) + specs(CTX_LEN),
        out_specs=out_specs,
        out_shape=out_shape,
        scratch_shapes=scratch,
        compiler_params=_params("parallel", "arbitrary"),
        name="mlstm_scan",
    )(*lat, *ctx)
    return outs if ctx_out else (outs[0], None)


def _mlstm_out_kernel(x_ref, mod_ref, hn_ref, a_ref, b_ref, wo_ref, fg_ref, out_ref, *, final):
    for r in range(OUT_ROWS // OUT_SUB_ROWS):
        rows = pl.ds(r * OUT_SUB_ROWS, OUT_SUB_ROWS)
        y = hn_ref[rows, :].astype(F32) * a_ref[rows, :].astype(F32) + b_ref[rows, :].astype(F32)
        xn = x_ref[rows, :] + mod_ref[:, 2 * D_MODEL:] * _dot(y.astype(BF16), wo_ref[...])
        if final:
            xn = xn * lax.rsqrt(jnp.mean(xn * xn, axis=-1, keepdims=True) + EPS) * fg_ref[...]
        out_ref[rows, :] = xn


def _mlstm_out(xs, mod, hn, a, b, w_out, final_g, *, tiles_per_batch, final):
    rows = xs.shape[0]
    row_spec = pl.BlockSpec((OUT_ROWS, D_MODEL), lambda i: (i, 0))
    inner = pl.BlockSpec((OUT_ROWS, D_INNER), lambda i: (i, 0))
    return pl.pallas_call(
        functools.partial(_mlstm_out_kernel, final=final),
        grid=(rows // OUT_ROWS,),
        in_specs=[
            row_spec,
            pl.BlockSpec((None, 1, 3 * D_MODEL), _mod_index(tiles_per_batch)),
            inner, inner, inner,
            pl.BlockSpec((D_INNER, D_MODEL), lambda i: (0, 0)),
            pl.BlockSpec((1, D_MODEL), lambda i: (0, 0)),
        ],
        out_specs=row_spec,
        out_shape=jax.ShapeDtypeStruct((rows, D_MODEL), F32),
        compiler_params=_params("parallel"),
        name="mlstm_out",
    )(xs, mod, hn, a, b, w_out, final_g)


def _gate_weights(w_gate, b_gate):
    def order(a):
        a = a.reshape(a.shape[:-1] + (4, HEADS))
        return jnp.concatenate([a[..., 0::2, :], a[..., 1::2, :]], axis=-2).reshape(a.shape[:-2] + (N_GATES,))

    wg = jnp.swapaxes(order(w_gate).reshape(3, HEADS, HEAD_DIM, N_GATES), -1, -2).astype(BF16)
    return wg, order(b_gate).reshape(N_GATES, 1)


def kernel(x, c, ctx, c_ctx, norm_g, mod_w, mod_b, conv_w_in, conv_w, conv_w_out, m_w_in, m_conv_w, m_wq,
           m_wk, m_wv, m_w_gate, m_b_gate, m_norm_g, m_skip, m_w_out, final_g):
    assert x.shape == (BATCH, SEQ, D_MODEL) and ctx.shape == (BATCH, CTX_LEN, D_MODEL)
    c_all = jnp.zeros((MOD_ROWS, D_MODEL), F32).at[:BATCH].set(c).at[CTX_MOD_ROW].set(c_ctx)
    mods = _modulation(c_all, mod_w, mod_b).reshape(DEPTH, MOD_ROWS, 1, 3 * D_MODEL)
    xs = x.reshape(BATCH * SEQ, D_MODEL)
    cs = ctx.reshape(BATCH * CTX_LEN, D_MODEL)
    lat_tiles = SEQ // ROW_TILE
    final_g2 = final_g.reshape(1, D_MODEL)
    for i in range(DEPTH):
        last = i == DEPTH - 1
        j = i // 2
        g = norm_g[i].reshape(1, D_MODEL)
        if i % 2 == 0:
            w_in, w_out = conv_w_in[j].astype(BF16), conv_w_out[j].astype(BF16)
            xs = _conv_layer(xs, mods[i], g, w_in, conv_w[j], w_out, period=GRID_W, tiles_per_batch=lat_tiles)
            if not last:
                cs = _conv_layer(cs, mods[i], g, w_in, conv_w[j], w_out, period=CTX_LEN, tiles_per_batch=None)
        else:
            w_in, w_out = m_w_in[j].astype(BF16), m_w_out[j].astype(BF16)
            wq, wk, wv = m_wq[j].astype(BF16), m_wk[j].astype(BF16), m_wv[j].astype(BF16)
            wg, bg = _gate_weights(m_w_gate[j], m_b_gate[j])
            proj = functools.partial(_mlstm_proj, norm_g=g, w_in=w_in, conv_w=m_conv_w[j], wq=wq, wk=wk, wv=wv,
                                     wg=wg, bg=bg, m_norm_g=m_norm_g[j].reshape(1, D_INNER),
                                     m_skip=m_skip[j].reshape(1, D_INNER))
            px = proj(xs, mods[i], period=GRID_W, tiles_per_batch=lat_tiles, gated=True)
            pc = proj(cs, mods[i], period=CTX_LEN, tiles_per_batch=None, gated=not last)

            def stream(p, tokens):
                qt, k, vt, gates = p[:4]
                t_shape = (BATCH, tokens // SCAN_CHUNK, D_INNER, SCAN_CHUNK)
                return [k.reshape(BATCH, tokens, D_INNER), qt.reshape(t_shape), vt.reshape(t_shape)] + list(
                    _gate_prep(gates, tokens))

            hnx, hnc = _scan(stream(px, SEQ), stream(pc, CTX_LEN), ctx_out=not last)
            out = functools.partial(_mlstm_out, w_out=w_out, final_g=final_g2)
            if not last:
                cs = out(cs, mods[i], hnc.reshape(-1, D_INNER), *pc[4:], tiles_per_batch=None, final=False)
            xs = out(xs, mods[i], hnx.reshape(-1, D_INNER), *px[4:], tiles_per_batch=SEQ // OUT_ROWS, final=last)
    return xs.reshape(BATCH, SEQ, D_MODEL)
```
